```python
import math
import jax, jax.numpy as jnp
from jax import lax
import numpy as np

D_MODEL = 2048
BATCH = 4
SEQ = 8192
DEPTH = 1
DEC_BATCH = 32
DEC_SEQ = 64
PAST_LEN = 2048

CHUNK = 64
ATTN_WIDTH = D_MODEL // 2
SSM_WIDTH = D_MODEL - ATTN_WIDTH
N_HEADS = 8
HEAD_DIM = ATTN_WIDTH // (2 * N_HEADS)
V_DIM = 2 * HEAD_DIM
ROT_DIM = HEAD_DIM // 4
ROPE_THETA = 500000.0
SSM_GROUP = 16
N_SSM_GROUPS = SSM_WIDTH // SSM_GROUP
SSM_STATE = 64
IN_WIDTH = 3 * ATTN_WIDTH + SSM_WIDTH
D_FF = 4 * D_MODEL
Q_BLOCK = 128
EPS = 1e-6
NEG_INF = -1e30

kernel_name = "hymba_diffattn_s5_streaming_step"


def rms_norm(x, g):
    xf = x.astype(jnp.float32)
    y = xf * lax.rsqrt(jnp.mean(xf * xf, axis=-1, keepdims=True) + EPS)
    return (y * g.astype(jnp.float32)).astype(x.dtype)


def ada_modulation(c, w_ada, b_ada):
    m = jax.nn.silu(c) @ w_ada + b_ada
    return [t[:, None, :] for t in jnp.split(m, 6, axis=-1)]


def rope_tables(positions):
    inv = jnp.power(jnp.float32(ROPE_THETA), -jnp.arange(0, ROT_DIM, 2, dtype=jnp.float32) / ROT_DIM)
    ang = positions.astype(jnp.float32)[:, None] * inv[None, :]
    return jnp.cos(ang), jnp.sin(ang)


def apply_partial_rope(t, cos, sin):
    half = ROT_DIM // 2
    c = cos[None, :, None, None, :]
    s = sin[None, :, None, None, :]
    t1 = t[..., :half].astype(jnp.float32)
    t2 = t[..., half:ROT_DIM].astype(jnp.float32)
    rot = jnp.concatenate([t1 * c - t2 * s, t2 * c + t1 * s], axis=-1).astype(t.dtype)
    return jnp.concatenate([rot, t[..., ROT_DIM:]], axis=-1)


def diff_attend(q, k, v, lam, mask=None):
    s = jnp.einsum('bqhcd,bkhcd->bhcqk', q.astype(jnp.float32), k.astype(jnp.float32)) * (HEAD_DIM ** -0.5)
    if mask is not None:
        s = jnp.where(mask, s, NEG_INF)
    p = jax.nn.softmax(s, axis=-1)
    w = p[:, :, 0] - lam * p[:, :, 1]
    return jnp.einsum('bhqk,bkhd->bqhd', w, v.astype(jnp.float32))


def prompt_diff_attention(q, k, v, lam):
    bsz, seq = q.shape[:2]
    n_blk = seq // Q_BLOCK
    q_blocks = q.reshape(bsz, n_blk, Q_BLOCK, N_HEADS, 2, HEAD_DIM).swapaxes(0, 1)
    key_chunk = jnp.arange(seq, dtype=jnp.int32) // CHUNK

    def one_block(args):
        blk, q_blk = args
        q_chunk = (blk * Q_BLOCK + jnp.arange(Q_BLOCK, dtype=jnp.int32)) // CHUNK
        mask = key_chunk[None, :] <= q_chunk[:, None]
        return diff_attend(q_blk, k, v, lam, mask)

    o = lax.map(one_block, (jnp.arange(n_blk, dtype=jnp.int32), q_blocks))
    return o.swapaxes(0, 1).reshape(bsz, seq, N_HEADS, V_DIM)


def s5_discretize(lam_re, lam_im, log_dt, b_re, b_im):
    lam_re = lam_re.astype(jnp.float32)
    lam_im = lam_im.astype(jnp.float32)
    dt = jnp.exp(log_dt.astype(jnp.float32))[:, None]
    mag = jnp.exp(lam_re * dt)
    ar = mag * jnp.cos(lam_im * dt)
    ai = mag * jnp.sin(lam_im * dt)
    den = lam_re * lam_re + lam_im * lam_im
    f_re = ((ar - 1.0) * lam_re + ai * lam_im) / den
    f_im = (ai * lam_re - (ar - 1.0) * lam_im) / den
    b_re = b_re.astype(jnp.float32)
    b_im = b_im.astype(jnp.float32)
    bb_re = f_re[..., None] * b_re - f_im[..., None] * b_im
    bb_im = f_re[..., None] * b_im + f_im[..., None] * b_re
    return ar, ai, bb_re, bb_im


def complex_affine_combine(e1, e2):
    a1r, a1i, b1r, b1i = e1
    a2r, a2i, b2r, b2i = e2
    return (a2r * a1r - a2i * a1i,
            a2r * a1i + a2i * a1r,
            a2r * b1r - a2i * b1i + b2r,
            a2r * b1i + a2i * b1r + b2i)


def s5_block(h_re, h_im, u_blk, ar, ai, bb_re, bb_im, c_re, c_im):
    bu_re = jnp.einsum('blgh,gph->blgp', u_blk, bb_re)
    bu_im = jnp.einsum('blgh,gph->blgp', u_blk, bb_im)
    bu_re = bu_re.at[:, 0].add(ar * h_re - ai * h_im)
    bu_im = bu_im.at[:, 0].add(ar * h_im + ai * h_re)
    a_re = jnp.broadcast_to(ar, bu_re.shape)
    a_im = jnp.broadcast_to(ai, bu_im.shape)
    _, _, s_re, s_im = lax.associative_scan(complex_affine_combine, (a_re, a_im, bu_re, bu_im), axis=1)
    y = (jnp.einsum('blgp,ghp->blgh', s_re, c_re.astype(jnp.float32))
         - jnp.einsum('blgp,ghp->blgh', s_im, c_im.astype(jnp.float32)))
    return s_re[:, -1], s_im[:, -1], y


def s5_prompt_scan(u_g, h0_re, h0_im, disc, c_re, c_im):
    bsz, seq = u_g.shape[:2]
    n_chunks = seq // CHUNK
    u_c = u_g.reshape(bsz, n_chunks, CHUNK, N_SSM_GROUPS, SSM_GROUP).swapaxes(0, 1)

    def step(carry, u_blk):
        h_re, h_im, y = s5_block(carry[0], carry[1], u_blk, *disc, c_re, c_im)
        return (h_re, h_im), y

    (h_re, h_im), ys = lax.scan(step, (h0_re, h0_im), u_c)
    return h_re, h_im, ys.swapaxes(0, 1).reshape(bsz, seq, N_SSM_GROUPS, SSM_GROUP)


def hybrid_layer(x, c, positions, past_k, past_v, h0_re, h0_im, lam_init,
                 w_ada, b_ada, g_mix, w_in, lam_q1, lam_k1, lam_q2, lam_k2, g_subln,
                 ssm_lam_re, ssm_lam_im, ssm_log_dt, ssm_b_re, ssm_b_im, ssm_c_re, ssm_c_im,
                 ssm_d, w_glu, b_glu, w_out, g_ffn, w_ff1, w_ff2):
    first_chunk = past_k is None
    bsz, seq, _ = x.shape
    shift1, scale1, gate1, shift2, scale2, gate2 = ada_modulation(c, w_ada, b_ada)

    h = rms_norm(x, g_mix) * (1.0 + scale1) + shift1
    proj = h @ w_in
    q, k, v, u = jnp.split(proj, [ATTN_WIDTH, 2 * ATTN_WIDTH, 3 * ATTN_WIDTH], axis=-1)

    cos, sin = rope_tables(positions)
    q = apply_partial_rope(q.reshape(bsz, seq, N_HEADS, 2, HEAD_DIM), cos, sin)
    k = apply_partial_rope(k.reshape(bsz, seq, N_HEADS, 2, HEAD_DIM), cos, sin)
    v = v.reshape(bsz, seq, N_HEADS, V_DIM)
    lam = (jnp.exp(jnp.sum(lam_q1.astype(jnp.float32) * lam_k1.astype(jnp.float32)))
           - jnp.exp(jnp.sum(lam_q2.astype(jnp.float32) * lam_k2.astype(jnp.float32)))
           + lam_init)
    if first_chunk:
        o = prompt_diff_attention(q, k, v, lam)
    else:
        o = diff_attend(q, jnp.concatenate([past_k, k.astype(past_k.dtype)], axis=1),
                        jnp.concatenate([past_v, v.astype(past_v.dtype)], axis=1), lam)
    o = rms_norm(o, g_subln) * (1.0 - lam_init)
    o = o.reshape(bsz, seq, ATTN_WIDTH).astype(x.dtype)

    disc = s5_discretize(ssm_lam_re, ssm_lam_im, ssm_log_dt, ssm_b_re, ssm_b_im)
    u_g = u.astype(jnp.float32).reshape(bsz, seq, N_SSM_GROUPS, SSM_GROUP)
    if first_chunk:
        zeros = jnp.zeros((bsz, N_SSM_GROUPS, SSM_STATE), jnp.float32)
        h_re, h_im, y_ssm = s5_prompt_scan(u_g, zeros, zeros, disc, ssm_c_re, ssm_c_im)
    else:
        h_re, h_im, y_ssm = s5_block(h0_re.astype(jnp.float32), h0_im.astype(jnp.float32),
                                     u_g, *disc, ssm_c_re, ssm_c_im)
    y_ssm = y_ssm + ssm_d.astype(jnp.float32).reshape(N_SSM_GROUPS, SSM_GROUP) * u_g
    z = jax.nn.gelu(y_ssm.reshape(bsz, seq, SSM_WIDTH)).astype(x.dtype)
    g = z @ w_glu + b_glu
    y_glu = g[..., :SSM_WIDTH] * jax.nn.sigmoid(g[..., SSM_WIDTH:])

    mix = jnp.concatenate([o, y_glu.astype(x.dtype)], axis=-1) @ w_out
    x = x + gate1 * mix

    h2 = rms_norm(x, g_ffn) * (1.0 + scale2) + shift2
    ff = jnp.square(jax.nn.relu(h2 @ w_ff1)) @ w_ff2
    x = x + gate2 * ff
    return x, k, v, h_re, h_im


def setup_inputs(seed: int = 0) -> dict:
    key = jax.random.key(seed)
    ks = iter(jax.random.split(key, 40))

    def nrm(shape, scale):
        return scale * jax.random.normal(next(ks), shape, jnp.float32)

    G, P, Hg = N_SSM_GROUPS, SSM_STATE, SSM_GROUP
    n_idx = jnp.arange(P, dtype=jnp.float32)
    return {
        "x_prompt": nrm((BATCH, SEQ, D_MODEL), 1.0),
        "x_sample": nrm((DEC_BATCH, DEC_SEQ, D_MODEL), 1.0),
        "c_prompt": nrm((BATCH, D_MODEL), 1.0),
        "c_sample": nrm((DEC_BATCH, D_MODEL), 1.0),
        "cache_k": nrm((DEPTH, DEC_BATCH, PAST_LEN, N_HEADS, 2, HEAD_DIM), 1.0),
        "cache_v": nrm((DEPTH, DEC_BATCH, PAST_LEN, N_HEADS, V_DIM), 1.0),
        "state_ssm_re": nrm((DEPTH, DEC_BATCH, G, P), 0.1),
        "state_ssm_im": nrm((DEPTH, DEC_BATCH, G, P), 0.1),
        "w_ada": nrm((DEPTH, D_MODEL, 6 * D_MODEL), 0.2 * D_MODEL ** -0.5),
        "b_ada": nrm((DEPTH, 6 * D_MODEL), 0.02),
        "g_mix": 1.0 + nrm((DEPTH, D_MODEL), 0.02),
        "w_in": nrm((DEPTH, D_MODEL, IN_WIDTH), D_MODEL ** -0.5),
        "lam_q1": nrm((DEPTH, HEAD_DIM), 0.1),
        "lam_k1": nrm((DEPTH, HEAD_DIM), 0.1),
        "lam_q2": nrm((DEPTH, HEAD_DIM), 0.1),
        "lam_k2": nrm((DEPTH, HEAD_DIM), 0.1),
        "g_subln": 1.0 + nrm((DEPTH, V_DIM), 0.02),
        "ssm_lam_re": -0.5 + nrm((DEPTH, G, P), 0.01),
        "ssm_lam_im": math.pi * n_idx + nrm((DEPTH, G, P), 0.01),
        "ssm_log_dt": jax.random.uniform(next(ks), (DEPTH, G), jnp.float32, math.log(1e-3), math.log(1e-1)),
        "ssm_b_re": nrm((DEPTH, G, P, Hg), (2 * Hg) ** -0.5),
        "ssm_b_im": nrm((DEPTH, G, P, Hg), (2 * Hg) ** -0.5),
        "ssm_c_re": nrm((DEPTH, G, Hg, P), (2 * P) ** -0.5),
        "ssm_c_im": nrm((DEPTH, G, Hg, P), (2 * P) ** -0.5),
        "ssm_d": nrm((DEPTH, SSM_WIDTH), 1.0),
        "w_glu": nrm((DEPTH, SSM_WIDTH, 2 * SSM_WIDTH), SSM_WIDTH ** -0.5),
        "b_glu": nrm((DEPTH, 2 * SSM_WIDTH), 0.02),
        "w_out": nrm((DEPTH, D_MODEL, D_MODEL), D_MODEL ** -0.5),
        "g_ffn": 1.0 + nrm((DEPTH, D_MODEL), 0.02),
        "w_ff1": nrm((DEPTH, D_MODEL, D_FF), D_MODEL ** -0.5),
        "w_ff2": nrm((DEPTH, D_FF, D_MODEL), D_FF ** -0.5),
        "g_final": 1.0 + nrm((D_MODEL,), 0.02),
    }


def reference(x_prompt, x_sample, c_prompt, c_sample, cache_k, cache_v, state_ssm_re, state_ssm_im,
              w_ada, b_ada, g_mix, w_in, lam_q1, lam_k1, lam_q2, lam_k2, g_subln,
              ssm_lam_re, ssm_lam_im, ssm_log_dt, ssm_b_re, ssm_b_im, ssm_c_re, ssm_c_im,
              ssm_d, w_glu, b_glu, w_out, g_ffn, w_ff1, w_ff2, g_final):
    xp, xs = x_prompt, x_sample
    pos_p = jnp.arange(xp.shape[1], dtype=jnp.int32)
    pos_s = cache_k.shape[2] + jnp.arange(xs.shape[1], dtype=jnp.int32)
    kp, vp, srp, sip, ksn, vsn, srs, sis = [], [], [], [], [], [], [], []
    for l in range(DEPTH):
        lam_init = 0.8 - 0.6 * math.exp(-0.3 * l)
        params = (w_ada[l], b_ada[l], g_mix[l], w_in[l], lam_q1[l], lam_k1[l], lam_q2[l], lam_k2[l],
                  g_subln[l], ssm_lam_re[l], ssm_lam_im[l], ssm_log_dt[l], ssm_b_re[l], ssm_b_im[l],
                  ssm_c_re[l], ssm_c_im[l], ssm_d[l], w_glu[l], b_glu[l], w_out[l], g_ffn[l],
                  w_ff1[l], w_ff2[l])
        xp, k_l, v_l, hr_l, hi_l = hybrid_layer(xp, c_prompt, pos_p, None, None, None, None,
                                                lam_init, *params)
        kp.append(k_l); vp.append(v_l); srp.append(hr_l); sip.append(hi_l)
        xs, k_l, v_l, hr_l, hi_l = hybrid_layer(xs, c_sample, pos_s, cache_k[l], cache_v[l],
                                                state_ssm_re[l], state_ssm_im[l], lam_init, *params)
        ksn.append(k_l); vsn.append(v_l); srs.append(hr_l); sis.append(hi_l)
    y_prompt = rms_norm(xp, g_final)
    y_sample = rms_norm(xs, g_final)
    return (y_prompt, y_sample,
            jnp.stack(kp), jnp.stack(vp), jnp.stack(srp), jnp.stack(sip),
            jnp.stack(ksn), jnp.stack(vsn), jnp.stack(srs), jnp.stack(sis))
```

```python
import functools
import math

import jax
import jax.numpy as jnp
from jax import lax
from jax.experimental import pallas as pl
from jax.experimental.pallas import tpu as pltpu

F32 = jnp.float32
BF16 = jnp.bfloat16

D_MODEL = 2048
ATTN_WIDTH = D_MODEL // 2
SSM_WIDTH = D_MODEL - ATTN_WIDTH
N_HEADS = 8
HEAD_DIM = ATTN_WIDTH // (2 * N_HEADS)
V_DIM = 2 * HEAD_DIM
ROT_DIM = HEAD_DIM // 4
ROPE_THETA = 500000.0
CHUNK = 64
SSM_GROUP = 16
N_GROUPS = SSM_WIDTH // SSM_GROUP
SSM_STATE = 64
IN_WIDTH = 3 * ATTN_WIDTH + SSM_WIDTH
D_FF = 4 * D_MODEL
EPS = 1e-6
NEG_INF = -1e30
LAM_INIT = 0.8 - 0.6 * math.exp(-0.3 * 0)

LANES = 128
S5_CHUNK = 32
S5_N = S5_CHUNK * SSM_GROUP
VMEM_LIMIT = 56 * 1024 * 1024

HIGHEST = lax.Precision.HIGHEST


def _params(sem, vmem=VMEM_LIMIT):
    return pltpu.CompilerParams(dimension_semantics=sem, vmem_limit_bytes=vmem)


def _sigmoid(x):
    return 1.0 / (1.0 + jnp.exp(-x))


def _ada_kernel(c_ref, w_ref, b_ref, o_ref):
    c = c_ref[...]
    s = (c * _sigmoid(c)).astype(BF16)
    o_ref[...] = jnp.dot(s, w_ref[...].astype(BF16), preferred_element_type=F32) + b_ref[...]


def _ada(c_all, w_ada, b_ada):
    nb, d = c_all.shape
    n = w_ada.shape[1]
    tn = 1024
    return pl.pallas_call(
        _ada_kernel,
        out_shape=jax.ShapeDtypeStruct((nb, n), F32),
        grid=(n // tn,),
        in_specs=[pl.BlockSpec((nb, d), lambda j: (0, 0)),
                  pl.BlockSpec((d, tn), lambda j: (0, j)),
                  pl.BlockSpec((1, tn), lambda j: (0, j))],
        out_specs=pl.BlockSpec((nb, tn), lambda j: (0, j)),
        compiler_params=_params(("parallel",)),
        name="ada_modulation",
    )(c_all, w_ada, b_ada.reshape(1, n))


def _rope_kernel(c_ref, sa_ref, sb_ref, *, offset):
    tl = c_ref.shape[0]
    row = lax.broadcasted_iota(jnp.int32, (tl, LANES), 0)
    lane = lax.broadcasted_iota(jnp.int32, (tl, LANES), 1)
    pos = (offset + pl.program_id(0) * tl + row).astype(F32)
    d = lane % HEAD_DIM
    fi = (d % (ROT_DIM // 2)).astype(F32)
    inv = jnp.exp(fi * (-2.0 * math.log(ROPE_THETA) / ROT_DIM))
    ang = pos * inv
    c = jnp.cos(ang)
    s = jnp.sin(ang)
    in_rot = d < ROT_DIM
    first = d < ROT_DIM // 2
    c_ref[...] = jnp.where(in_rot, c, 1.0)
    sa_ref[...] = jnp.where(in_rot, jnp.where(first, 0.0, s), 0.0)
    sb_ref[...] = jnp.where(first, -s, 0.0)


def _rope_tables(length, offset):
    tl = min(length, 1024)
    shp = jax.ShapeDtypeStruct((length, LANES), F32)
    spec = pl.BlockSpec((tl, LANES), lambda i: (i, 0))
    return pl.pallas_call(
        functools.partial(_rope_kernel, offset=offset),
        out_shape=(shp, shp, shp),
        grid=(length // tl,),
        out_specs=(spec, spec, spec),
        compiler_params=_params(("parallel",)),
        name="rope_tables",
    )()


def _modulated_norm(x, g, scale, shift):
    ms = jnp.mean(x * x, axis=-1, keepdims=True)
    y = x * lax.rsqrt(ms + EPS) * g
    return y * (1.0 + scale) + shift


def _inproj_kernel(x_ref, sc_ref, sh_ref, g_ref, w_ref, c_ref, sa_ref, sb_ref,
                   q_ref, k_ref, v_ref, u_ref, kb_ref, vb_ref):
    nb, tl, d = x_ref.shape
    rows = nb * tl
    h = _modulated_norm(x_ref[...], g_ref[...], sc_ref[...], sh_ref[...])
    hb = h.reshape(rows, d).astype(BF16)
    cos = c_ref[...]
    sin_a = sa_ref[...]
    sin_b = sb_ref[...]

    def rope(t):
        ra = pltpu.roll(t, ROT_DIM // 2, 1).reshape(nb, tl, LANES)
        rb = pltpu.roll(t, LANES - ROT_DIM // 2, 1).reshape(nb, tl, LANES)
        return t.reshape(nb, tl, LANES) * cos + ra * sin_a + rb * sin_b

    q = jnp.dot(hb, w_ref[:, 0:ATTN_WIDTH], preferred_element_type=F32)
    for hd in range(N_HEADS):
        sl = slice(hd * LANES, (hd + 1) * LANES)
        q_ref[:, :, sl] = (rope(q[:, sl]) * (HEAD_DIM ** -0.5)).astype(BF16)
    k = jnp.dot(hb, w_ref[:, ATTN_WIDTH:2 * ATTN_WIDTH], preferred_element_type=F32)
    for hd in range(N_HEADS):
        sl = slice(hd * LANES, (hd + 1) * LANES)
        kr = rope(k[:, sl])
        k_ref[:, :, sl] = kr
        kb_ref[:, :, sl] = kr.astype(BF16)
    v = jnp.dot(hb, w_ref[:, 2 * ATTN_WIDTH:3 * ATTN_WIDTH], preferred_element_type=F32)
    v = v.reshape(nb, tl, ATTN_WIDTH)
    v_ref[...] = v
    vb_ref[...] = v.astype(BF16)
    u = jnp.dot(hb, w_ref[:, 3 * ATTN_WIDTH:], preferred_element_type=F32)
    u_ref[...] = u.reshape(nb, tl, SSM_WIDTH)


def _inproj(x, scale, shift, g_mix, w_in_b, tables, nb, tl):
    b, l, d = x.shape
    grid = (b // nb, l // tl)
    xspec = lambda w: pl.BlockSpec((nb, tl, w), lambda i, j: (i, j, 0))
    mspec = pl.BlockSpec((nb, 1, d), lambda i, j: (i, 0, 0))
    tspec = pl.BlockSpec((tl, LANES), lambda i, j: (j, 0))
    out = lambda dt: jax.ShapeDtypeStruct((b, l, ATTN_WIDTH), dt)
    return pl.pallas_call(
        _inproj_kernel,
        out_shape=(out(BF16), out(F32), out(F32), out(F32), out(BF16), out(BF16)),
        grid=grid,
        in_specs=[xspec(d), mspec, mspec,
                  pl.BlockSpec((1, d), lambda i, j: (0, 0)),
                  pl.BlockSpec((d, IN_WIDTH), lambda i, j: (0, 0), pipeline_mode=pl.Buffered(1)),
                  tspec, tspec, tspec],
        out_specs=tuple(xspec(ATTN_WIDTH) for _ in range(6)),
        compiler_params=_params(("parallel", "parallel")),
        name="norm_inproj_rope",
    )(x, scale, shift, g_mix.reshape(1, d), w_in_b, *tables)


def _stack_components(q):
    lane = lax.broadcasted_iota(jnp.int32, q.shape, 1)
    zero = jnp.zeros_like(q)
    return jnp.concatenate([jnp.where(lane < HEAD_DIM, q, zero),
                            jnp.where(lane >= HEAD_DIM, q, zero)], axis=0)


def _diff_epilogue(acc, l, tq, lq1, lk1, lq2, lk2, g_subln):
    lam = (jnp.exp(jnp.sum(lq1 * lk1, axis=-1, keepdims=True))
           - jnp.exp(jnp.sum(lq2 * lk2, axis=-1, keepdims=True)) + LAM_INIT)
    o = acc[:tq] / l[:tq] - lam * (acc[tq:] / l[tq:])
    ms = jnp.mean(o * o, axis=-1, keepdims=True)
    return o * lax.rsqrt(ms + EPS) * g_subln * (1.0 - LAM_INIT)


def _nt_dot(a, b):
    return lax.dot_general(a, b, (((1,), (1,)), ((), ())), preferred_element_type=F32)


def _attn_prompt_kernel(lq1_ref, lk1_ref, lq2_ref, lk2_ref, gs_ref, q_ref, k_ref, v_ref, o_ref, *, tq):
    i = pl.program_id(2)
    qs = _stack_components(q_ref[0])

    def step(j, carry, masked):
        m, l, acc = carry
        start = pl.multiple_of(j * tq, tq)
        k = k_ref[0, pl.ds(start, tq), :]
        v = v_ref[0, pl.ds(start, tq), :]
        s = _nt_dot(qs, k)
        if masked:
            r = lax.broadcasted_iota(jnp.int32, s.shape, 0)
            c = lax.broadcasted_iota(jnp.int32, s.shape, 1)
            qc = jnp.where(r >= tq, r - tq, r) // CHUNK
            s = jnp.where(c // CHUNK <= qc, s, NEG_INF)
        m_new = jnp.maximum(m, jnp.max(s, axis=-1, keepdims=True))
        alpha = jnp.exp(m - m_new)
        p = jnp.exp(s - m_new)
        l = alpha * l + jnp.sum(p, axis=-1, keepdims=True)
        acc = alpha * acc + jnp.dot(p.astype(BF16), v, preferred_element_type=F32)
        return m_new, l, acc

    init = (jnp.full((2 * tq, 1), NEG_INF, F32), jnp.zeros((2 * tq, 1), F32),
            jnp.zeros((2 * tq, V_DIM), F32))
    carry = lax.fori_loop(0, i, lambda j, c: step(j, c, False), init)
    _, l, acc = step(i, carry, True)
    o = _diff_epilogue(acc, l, tq, lq1_ref[...], lk1_ref[...], lq2_ref[...], lk2_ref[...], gs_ref[...])
    o_ref[0] = o.astype(BF16)


def _attn_prompt(q, kb, vb, lam_vecs, g_subln, tq):
    b, l, _ = q.shape
    vec = pl.BlockSpec((1, HEAD_DIM), lambda bi, h, i: (0, 0))
    return pl.pallas_call(
        functools.partial(_attn_prompt_kernel, tq=tq),
        out_shape=jax.ShapeDtypeStruct((b, l, ATTN_WIDTH), BF16),
        grid=(b, N_HEADS, l // tq),
        in_specs=[vec, vec, vec, vec,
                  pl.BlockSpec((1, V_DIM), lambda bi, h, i: (0, 0)),
                  pl.BlockSpec((1, tq, LANES), lambda bi, h, i: (bi, i, h)),
                  pl.BlockSpec((1, l, LANES), lambda bi, h, i: (bi, 0, h)),
                  pl.BlockSpec((1, l, LANES), lambda bi, h, i: (bi, 0, h))],
        out_specs=pl.BlockSpec((1, tq, LANES), lambda bi, h, i: (bi, i, h)),
        compiler_params=_params(("parallel", "parallel", "arbitrary")),
        name="prompt_diff_attention",
    )(*lam_vecs, g_subln.reshape(1, V_DIM), q, kb, vb)


def _attn_sample_kernel(lq1_ref, lk1_ref, lq2_ref, lk2_ref, gs_ref, q_ref, ck_ref, cv_ref,
                        kn_ref, vn_ref, o_ref):
    tq = q_ref.shape[1]
    qs = _stack_components(q_ref[0])
    kp = ck_ref[0].astype(BF16)
    vp = cv_ref[0].astype(BF16)
    s_p = _nt_dot(qs, kp)
    s_n = _nt_dot(qs, kn_ref[0])
    m = jnp.maximum(jnp.max(s_p, axis=-1, keepdims=True), jnp.max(s_n, axis=-1, keepdims=True))
    p_p = jnp.exp(s_p - m)
    p_n = jnp.exp(s_n - m)
    l = jnp.sum(p_p, axis=-1, keepdims=True) + jnp.sum(p_n, axis=-1, keepdims=True)
    acc = (jnp.dot(p_p.astype(BF16), vp, preferred_element_type=F32)
           + jnp.dot(p_n.astype(BF16), vn_ref[0], preferred_element_type=F32))
    o = _diff_epilogue(acc, l, tq, lq1_ref[...], lk1_ref[...], lq2_ref[...], lk2_ref[...], gs_ref[...])
    o_ref[0] = o.astype(BF16)


def _attn_sample(q, cache_k, cache_v, kb, vb, lam_vecs, g_subln):
    b, l, _ = q.shape
    past = cache_k.shape[1]
    vec = pl.BlockSpec((1, HEAD_DIM), lambda bi, h: (0, 0))
    new = pl.BlockSpec((1, l, LANES), lambda bi, h: (bi, 0, h))
    old = pl.BlockSpec((1, past, LANES), lambda bi, h: (bi, 0, h))
    return pl.pallas_call(
        _attn_sample_kernel,
        out_shape=jax.ShapeDtypeStruct((b, l, ATTN_WIDTH), BF16),
        grid=(b, N_HEADS),
        in_specs=[vec, vec, vec, vec, pl.BlockSpec((1, V_DIM), lambda bi, h: (0, 0)),
                  new, old, old, new, new],
        out_specs=new,
        compiler_params=_params(("parallel", "parallel")),
        name="sample_diff_attention",
    )(*lam_vecs, g_subln.reshape(1, V_DIM), q, cache_k, cache_v, kb, vb)


def _s5_discretize(lr, li, dt):
    mag = jnp.exp(lr * dt)
    ar = mag * jnp.cos(li * dt)
    ai = mag * jnp.sin(li * dt)
    den = lr * lr + li * li
    f_re = ((ar - 1.0) * lr + ai * li) / den
    f_im = (ai * lr - (ar - 1.0) * li) / den
    return f_re, f_im


def _s5_power(lr, li, dt, e):
    mag = jnp.exp(lr * dt * e)
    ang = li * dt * e
    return mag * jnp.cos(ang), mag * jnp.sin(ang)


def _s5_prep_kernel(lrr_ref, lir_ref, lrc_ref, lic_ref, ldt_ref, brt_ref, bit_ref, crt_ref, cit_ref,
                    mt_ref, w_ref, v_ref, a_ref):
    p = SSM_STATE
    lrr, lir = lrr_ref[0], lir_ref[0]
    lrc, lic = lrc_ref[0], lic_ref[0]
    dt = jnp.exp(ldt_ref[0])
    brt, bit = brt_ref[0], bit_ref[0]
    crt, cit = crt_ref[0], cit_ref[0]

    f_re, f_im = _s5_discretize(lrr, lir, dt)
    bbt_re = f_re * brt - f_im * bit
    bbt_im = f_re * bit + f_im * brt

    t_re = jnp.concatenate([bbt_re] * S5_CHUNK, axis=0)
    t_im = jnp.concatenate([bbt_im] * S5_CHUNK, axis=0)
    s_idx = lax.broadcasted_iota(jnp.int32, (S5_N, 1), 0) // SSM_GROUP
    pr, pi = _s5_power(lrr, lir, dt, (S5_CHUNK - 1 - s_idx).astype(F32))
    w_ref[0] = jnp.concatenate([pr * t_re - pi * t_im, pr * t_im + pi * t_re], axis=1)

    hh = lax.broadcasted_iota(jnp.int32, (SSM_GROUP, S5_N), 0)
    ll = lax.broadcasted_iota(jnp.int32, (SSM_GROUP, S5_N), 1)
    expand = (ll % SSM_GROUP == hh).astype(F32)
    ct_re = jnp.dot(crt, expand, precision=HIGHEST, preferred_element_type=F32)
    ct_im = jnp.dot(cit, expand, precision=HIGHEST, preferred_element_type=F32)
    t_idx = (lax.broadcasted_iota(jnp.int32, (1, S5_N), 1) // SSM_GROUP).astype(F32)

    pr, pi = _s5_power(lrc, lic, dt, t_idx + 1.0)
    v_ref[0, :p, :] = ct_re * pr - ct_im * pi
    v_ref[0, p:, :] = -(ct_re * pi + ct_im * pr)

    pr, pi = _s5_power(lrc, lic, dt, t_idx)
    vj_re = ct_re * pr - ct_im * pi
    vj_im = ct_re * pi + ct_im * pr
    kf = (jnp.dot(bbt_re, vj_re, precision=HIGHEST, preferred_element_type=F32)
          - jnp.dot(bbt_im, vj_im, precision=HIGHEST, preferred_element_type=F32))
    lane = lax.broadcasted_iota(jnp.int32, (SSM_GROUP, S5_N), 1)
    for s in range(S5_CHUNK):
        blk = kf if s == 0 else pltpu.roll(kf, SSM_GROUP * s, 1)
        blk = jnp.where(lane >= SSM_GROUP * s, blk, 0.0)
        mt_ref[0, s * SSM_GROUP:(s + 1) * SSM_GROUP, :] = blk

    ar, ai = _s5_power(lrr, lir, dt, float(S5_CHUNK))
    a_ref[0] = jnp.concatenate([ar, ai], axis=1)


def _s5_prep(lam_re, lam_im, log_dt, b_re, b_im, c_re, c_im):
    g, p = lam_re.shape
    row = pl.BlockSpec((1, 1, p), lambda i: (i, 0, 0))
    col = pl.BlockSpec((1, p, 1), lambda i: (i, 0, 0))
    bt = pl.BlockSpec((1, SSM_GROUP, p), lambda i: (i, 0, 0))
    ct = pl.BlockSpec((1, p, SSM_GROUP), lambda i: (i, 0, 0))
    return pl.pallas_call(
        _s5_prep_kernel,
        out_shape=(jax.ShapeDtypeStruct((g, S5_N, S5_N), F32),
                   jax.ShapeDtypeStruct((g, S5_N, 2 * p), F32),
                   jax.ShapeDtypeStruct((g, 2 * p, S5_N), F32),
                   jax.ShapeDtypeStruct((g, 1, 2 * p), F32)),
        grid=(g,),
        in_specs=[row, row, col, col, pl.BlockSpec((1, 1, 1), lambda i: (i, 0, 0)), bt, bt, ct, ct],
        out_specs=(pl.BlockSpec((1, S5_N, S5_N), lambda i: (i, 0, 0)),
                   pl.BlockSpec((1, S5_N, 2 * p), lambda i: (i, 0, 0)),
                   pl.BlockSpec((1, 2 * p, S5_N), lambda i: (i, 0, 0)),
                   pl.BlockSpec((1, 1, 2 * p), lambda i: (i, 0, 0))),
        compiler_params=_params(("parallel",)),
        name="s5_chunk_operators",
    )(lam_re.reshape(g, 1, p), lam_im.reshape(g, 1, p), lam_re.reshape(g, p, 1), lam_im.reshape(g, p, 1),
      log_dt.reshape(g, 1, 1), jnp.swapaxes(b_re, 1, 2), jnp.swapaxes(b_im, 1, 2),
      jnp.swapaxes(c_re, 1, 2), jnp.swapaxes(c_im, 1, 2))


def _s5_local_kernel(x_ref, w_ref, d_ref):
    d_ref[0] = jnp.dot(x_ref[0], w_ref[0].astype(BF16), preferred_element_type=F32)


def _s5_local(x, w):
    g, r, n = x.shape
    return pl.pallas_call(
        _s5_local_kernel,
        out_shape=jax.ShapeDtypeStruct((g, r, 2 * SSM_STATE), F32),
        grid=(g,),
        in_specs=[pl.BlockSpec((1, r, n), lambda i: (i, 0, 0)),
                  pl.BlockSpec((1, n, 2 * SSM_STATE), lambda i: (i, 0, 0))],
        out_specs=pl.BlockSpec((1, r, 2 * SSM_STATE), lambda i: (i, 0, 0)),
        compiler_params=_params(("parallel",)),
        name="s5_chunk_state_increment",
    )(x, w)


def _s5_scan_kernel(d_ref, h0_ref, a_ref, hs_ref, hf_ref, carry_ref):
    @pl.when(pl.program_id(0) == 0)
    def _():
        carry_ref[...] = h0_ref[...]

    a = a_ref[...]
    a_sw = pltpu.roll(a, SSM_STATE, 1)
    lane = lax.broadcasted_iota(jnp.int32, a.shape, 1)
    a_rr = jnp.where(lane < SSM_STATE, a, a_sw)
    a_ii = jnp.where(lane < SSM_STATE, -a_sw, a)

    def body(c, h):
        hs_ref[c] = h
        return h * a_rr + pltpu.roll(h, SSM_STATE, 1) * a_ii + d_ref[c]

    h = lax.fori_loop(0, d_ref.shape[0], body, carry_ref[...])
    carry_ref[...] = h
    hf_ref[...] = h


def _s5_scan(d, h0, a_rows, tc):
    nc, r2, w = d.shape
    return pl.pallas_call(
        _s5_scan_kernel,
        out_shape=(jax.ShapeDtypeStruct((nc, r2, w), F32), jax.ShapeDtypeStruct((r2, w), F32)),
        grid=(nc // tc,),
        in_specs=[pl.BlockSpec((tc, r2, w), lambda i: (i, 0, 0)),
                  pl.BlockSpec((r2, w), lambda i: (0, 0)),
                  pl.BlockSpec((r2, w), lambda i: (0, 0))],
        out_specs=(pl.BlockSpec((tc, r2, w), lambda i: (i, 0, 0)),
                   pl.BlockSpec((r2, w), lambda i: (0, 0))),
        scratch_shapes=[pltpu.VMEM((r2, w), F32)],
        compiler_params=_params(("arbitrary",)),
        name="s5_chunk_state_scan",
    )(d, h0, a_rows)


def _s5_out_kernel(x_ref, hs_ref, mt_ref, v_ref, y_ref):
    y_ref[0] = (jnp.dot(x_ref[0], mt_ref[0].astype(BF16), preferred_element_type=F32)
                + jnp.dot(hs_ref[0], v_ref[0].astype(BF16), preferred_element_type=F32))


def _s5_out(x, hs, mt, v):
    g, r, n = x.shape
    return pl.pallas_call(
        _s5_out_kernel,
        out_shape=jax.ShapeDtypeStruct((g, r, n), F32),
        grid=(g,),
        in_specs=[pl.BlockSpec((1, r, n), lambda i: (i, 0, 0)),
                  pl.BlockSpec((1, r, 2 * SSM_STATE), lambda i: (i, 0, 0)),
                  pl.BlockSpec((1, n, n), lambda i: (i, 0, 0)),
                  pl.BlockSpec((1, 2 * SSM_STATE, n), lambda i: (i, 0, 0))],
        out_specs=pl.BlockSpec((1, r, n), lambda i: (i, 0, 0)),
        compiler_params=_params(("parallel",)),
        name="s5_chunk_output",
    )(x, hs, mt, v)


def _s5(u, h0_re, h0_im, ops):
    mt, w, v, a = ops
    b, l, _ = u.shape
    g, p = N_GROUPS, SSM_STATE
    nc = l // S5_CHUNK
    x = u.reshape(b, nc, S5_CHUNK, g, SSM_GROUP).transpose(3, 0, 1, 2, 4)
    x = x.reshape(g, b * nc, S5_N).astype(BF16)
    d = _s5_local(x, w)
    d = d.reshape(g, b, nc, 2 * p).transpose(2, 1, 0, 3).reshape(nc, b * g, 2 * p)
    if h0_re is None:
        h0 = jnp.zeros((b * g, 2 * p), F32)
    else:
        h0 = jnp.concatenate([h0_re, h0_im], axis=-1).astype(F32).reshape(b * g, 2 * p)
    a_rows = jnp.broadcast_to(a.reshape(1, g, 2 * p), (b, g, 2 * p)).reshape(b * g, 2 * p)
    hs, hf = _s5_scan(d, h0, a_rows, tc=min(nc, 32))
    hs = hs.reshape(nc, b, g, 2 * p).transpose(2, 1, 0, 3).reshape(g, b * nc, 2 * p).astype(BF16)
    y = _s5_out(x, hs, mt, v)
    y = y.reshape(g, b, nc, S5_CHUNK, SSM_GROUP).transpose(1, 2, 3, 0, 4).reshape(b, l, SSM_WIDTH)
    hf = hf.reshape(b, g, 2 * p)
    return y, hf[..., :p], hf[..., p:]


def _gelu_tanh(x):
    return x * (0.5 * (1.0 + jnp.tanh(math.sqrt(2.0 / math.pi) * (x + 0.044715 * (x * x * x)))))


def _mix_kernel(x_ref, y_ref, u_ref, o_ref, gate_ref, d_ref, wg_ref, bg_ref, wo_ref, out_ref):
    nb, tl, d = x_ref.shape
    rows = nb * tl
    y = y_ref[...].reshape(rows, SSM_WIDTH) + d_ref[...] * u_ref[...].reshape(rows, SSM_WIDTH)
    z = _gelu_tanh(y).astype(BF16)
    gl = jnp.dot(z, wg_ref[...], preferred_element_type=F32) + bg_ref[...]
    yg = (gl[:, :SSM_WIDTH] * _sigmoid(gl[:, SSM_WIDTH:])).astype(BF16)
    mix = (jnp.dot(o_ref[...].reshape(rows, ATTN_WIDTH), wo_ref[:ATTN_WIDTH, :], preferred_element_type=F32)
           + jnp.dot(yg, wo_ref[ATTN_WIDTH:, :], preferred_element_type=F32))
    out_ref[...] = x_ref[...] + gate_ref[...] * mix.reshape(nb, tl, d)


def _mix(x, y_ssm, u, o_attn, gate, ssm_d, w_glu_b, b_glu, w_out_b, nb, tl):
    b, l, d = x.shape
    xspec = lambda w: pl.BlockSpec((nb, tl, w), lambda i, j: (i, j, 0))
    const = lambda shape: pl.BlockSpec(shape, lambda i, j: (0, 0), pipeline_mode=pl.Buffered(1))
    return pl.pallas_call(
        _mix_kernel,
        out_shape=jax.ShapeDtypeStruct((b, l, d), F32),
        grid=(b // nb, l // tl),
        in_specs=[xspec(d), xspec(SSM_WIDTH), xspec(SSM_WIDTH), xspec(ATTN_WIDTH),
                  pl.BlockSpec((nb, 1, d), lambda i, j: (i, 0, 0)),
                  const((1, SSM_WIDTH)), const((SSM_WIDTH, 2 * SSM_WIDTH)), const((1, 2 * SSM_WIDTH)),
                  const((d, d))],
        out_specs=xspec(d),
        compiler_params=_params(("parallel", "parallel")),
        name="glu_outproj_residual",
    )(x, y_ssm, u, o_attn, gate, ssm_d.reshape(1, SSM_WIDTH), w_glu_b, b_glu.reshape(1, -1), w_out_b)


def _ffn_kernel(x_ref, sc_ref, sh_ref, gate_ref, g_ref, gf_ref, w1_ref, w2_ref, y_ref, h_ref, acc_ref):
    nb, tl, d = x_ref.shape
    j = pl.program_id(2)

    @pl.when(j == 0)
    def _():
        h = _modulated_norm(x_ref[...], g_ref[...], sc_ref[...], sh_ref[...])
        h_ref[...] = h.reshape(nb * tl, d).astype(BF16)
        acc_ref[...] = jnp.zeros_like(acc_ref)

    hid = jnp.dot(h_ref[...], w1_ref[...], preferred_element_type=F32)
    hid = jnp.square(jnp.maximum(hid, 0.0)).astype(BF16)
    acc_ref[...] += jnp.dot(hid, w2_ref[...], preferred_element_type=F32)

    @pl.when(j == pl.num_programs(2) - 1)
    def _():
        x = x_ref[...] + gate_ref[...] * acc_ref[...].reshape(nb, tl, d)
        ms = jnp.mean(x * x, axis=-1, keepdims=True)
        y_ref[...] = x * lax.rsqrt(ms + EPS) * gf_ref[...]


def _ffn(x, scale, shift, gate, g_ffn, g_final, w1_b, w2_b, nb, tl, tf):
    b, l, d = x.shape
    xspec = pl.BlockSpec((nb, tl, d), lambda i, j, k: (i, j, 0))
    mspec = pl.BlockSpec((nb, 1, d), lambda i, j, k: (i, 0, 0))
    gspec = pl.BlockSpec((1, d), lambda i, j, k: (0, 0))
    return pl.pallas_call(
        _ffn_kernel,
        out_shape=jax.ShapeDtypeStruct((b, l, d), F32),
        grid=(b // nb, l // tl, D_FF // tf),
        in_specs=[xspec, mspec, mspec, mspec, gspec, gspec,
                  pl.BlockSpec((d, tf), lambda i, j, k: (0, k)),
                  pl.BlockSpec((tf, d), lambda i, j, k: (k, 0))],
        out_specs=xspec,
        scratch_shapes=[pltpu.VMEM((nb * tl, d), BF16), pltpu.VMEM((nb * tl, d), F32)],
        compiler_params=_params(("parallel", "parallel", "arbitrary")),
        name="relu2_mlp_final_norm",
    )(x, scale, shift, gate, g_ffn.reshape(1, d), g_final.reshape(1, d), w1_b, w2_b)


def _layer(x, mods, tables, past_k, past_v, h0_re, h0_im, wts, nb, tl, tq):
    shift1, scale1, gate1, shift2, scale2, gate2 = mods
    q, k, v, u, kb, vb = _inproj(x, scale1, shift1, wts["g_mix"], wts["w_in"], tables, nb, tl)
    if past_k is None:
        o = _attn_prompt(q, kb, vb, wts["lam_vecs"], wts["g_subln"], tq)
    else:
        o = _attn_sample(q, past_k, past_v, kb, vb, wts["lam_vecs"], wts["g_subln"])
    y_ssm, h_re, h_im = _s5(u, h0_re, h0_im, wts["s5_ops"])
    x1 = _mix(x, y_ssm, u, o, gate1, wts["ssm_d"], wts["w_glu"], wts["b_glu"], wts["w_out"], nb, tl)
    y = _ffn(x1, scale2, shift2, gate2, wts["g_ffn"], wts["g_final"], wts["w_ff1"], wts["w_ff2"],
             nb, tl, tf=512)
    b, l, _ = x.shape
    k6 = k.reshape(1, b, l, N_HEADS, 2, HEAD_DIM)
    v5 = v.reshape(1, b, l, N_HEADS, V_DIM)
    return y, k6, v5, h_re[None], h_im[None]


def kernel(x_prompt, x_sample, c_prompt, c_sample, cache_k, cache_v, state_ssm_re, state_ssm_im, w_ada, b_ada, g_mix, w_in, lam_q1, lam_k1, lam_q2, lam_k2, g_subln, ssm_lam_re, ssm_lam_im, ssm_log_dt, ssm_b_re, ssm_b_im, ssm_c_re, ssm_c_im, ssm_d, w_glu, b_glu, w_out, g_ffn, w_ff1, w_ff2, g_final):
    bp, lp, d = x_prompt.shape
    bs, ls, _ = x_sample.shape
    past = cache_k.shape[2]

    m = _ada(jnp.concatenate([c_prompt, c_sample], axis=0), w_ada[0], b_ada[0])
    mods = [m[:, i * d:(i + 1) * d].reshape(bp + bs, 1, d) for i in range(6)]
    mods_p = [t[:bp] for t in mods]
    mods_s = [t[bp:] for t in mods]

    wts = {
        "g_mix": g_mix[0], "w_in": w_in[0].astype(BF16),
        "lam_vecs": tuple(t[0].reshape(1, HEAD_DIM) for t in (lam_q1, lam_k1, lam_q2, lam_k2)),
        "g_subln": g_subln[0],
        "s5_ops": _s5_prep(ssm_lam_re[0], ssm_lam_im[0], ssm_log_dt[0], ssm_b_re[0], ssm_b_im[0],
                           ssm_c_re[0], ssm_c_im[0]),
        "ssm_d": ssm_d[0], "w_glu": w_glu[0].astype(BF16), "b_glu": b_glu[0],
        "w_out": w_out[0].astype(BF16), "g_ffn": g_ffn[0], "g_final": g_final,
        "w_ff1": w_ff1[0].astype(BF16), "w_ff2": w_ff2[0].astype(BF16),
    }

    tl_p = min(lp, 512)
    yp, kp, vp, srp, sip = _layer(x_prompt, mods_p, _rope_tables(lp, 0), None, None, None, None, wts,
                                  nb=1, tl=tl_p, tq=min(lp, 256))
    nb_s = max(1, min(bs, 512 // ls))
    ck = cache_k[0].reshape(bs, past, ATTN_WIDTH)
    cv = cache_v[0].reshape(bs, past, ATTN_WIDTH)
    ys, ks, vs, srs, sis = _layer(x_sample, mods_s, _rope_tables(ls, past), ck, cv,
                                  state_ssm_re[0], state_ssm_im[0], wts, nb=nb_s, tl=ls, tq=ls)
    return (yp, ys, kp, vp, srp, sip, ks, vs, srs, sis)
```

```python
import functools
import math

import jax
import jax.numpy as jnp
from jax import lax
from jax.experimental import pallas as pl
from jax.experimental.pallas import tpu as pltpu

F32 = jnp.float32
BF16 = jnp.bfloat16

D_MODEL = 2048
ATTN_WIDTH = D_MODEL // 2
SSM_WIDTH = D_MODEL - ATTN_WIDTH
N_HEADS = 8
HEAD_DIM = ATTN_WIDTH // (2 * N_HEADS)
V_DIM = 2 * HEAD_DIM
ROT_DIM = HEAD_DIM // 4
ROPE_THETA = 500000.0
CHUNK = 64
SSM_GROUP = 16
N_GROUPS = SSM_WIDTH // SSM_GROUP
SSM_STATE = 64
IN_WIDTH = 3 * ATTN_WIDTH + SSM_WIDTH
D_FF = 4 * D_MODEL
EPS = 1e-6
NEG_INF = -1e30
LAM_INIT = 0.8 - 0.6 * math.exp(-0.3 * 0)

LANES = 128
S5_CHUNK = 32
S5_N = S5_CHUNK * SSM_GROUP
VMEM_LIMIT = 56 * 1024 * 1024

HIGHEST = lax.Precision.HIGHEST


def _params(sem, vmem=VMEM_LIMIT):
    return pltpu.CompilerParams(dimension_semantics=sem, vmem_limit_bytes=vmem)


def _sigmoid(x):
    return 1.0 / (1.0 + jnp.exp(-x))


def _ada_kernel(c_ref, w_ref, b_ref, o_ref):
    c = c_ref[...]
    s = (c * _sigmoid(c)).astype(BF16)
    o_ref[...] = jnp.dot(s, w_ref[...].astype(BF16), preferred_element_type=F32) + b_ref[...]


def _ada(c_all, w_ada, b_ada):
    nb, d = c_all.shape
    n = w_ada.shape[1]
    tn = 1024
    return pl.pallas_call(
        _ada_kernel,
        out_shape=jax.ShapeDtypeStruct((nb, n), F32),
        grid=(n // tn,),
        in_specs=[pl.BlockSpec((nb, d), lambda j: (0, 0)),
                  pl.BlockSpec((d, tn), lambda j: (0, j)),
                  pl.BlockSpec((1, tn), lambda j: (0, j))],
        out_specs=pl.BlockSpec((nb, tn), lambda j: (0, j)),
        compiler_params=_params(("parallel",)),
        name="ada_modulation",
    )(c_all, w_ada, b_ada.reshape(1, n))


def _rope_angles(pos, freq_index):
    inv = jnp.exp(freq_index * (-2.0 * math.log(ROPE_THETA) / ROT_DIM))
    return pos * inv


def _rope_kernel(c_ref, sa_ref, sb_ref, *, offset):
    tl = c_ref.shape[0]
    row = lax.broadcasted_iota(jnp.int32, (tl, LANES), 0)
    lane = lax.broadcasted_iota(jnp.int32, (tl, LANES), 1)
    pos = (offset + pl.program_id(0) * tl + row).astype(F32)
    d = lane % HEAD_DIM
    ang = _rope_angles(pos, (d % (ROT_DIM // 2)).astype(F32))
    c = jnp.cos(ang)
    s = jnp.sin(ang)
    in_rot = d < ROT_DIM
    first = d < ROT_DIM // 2
    c_ref[...] = jnp.where(in_rot, c, 1.0)
    sa_ref[...] = jnp.where(in_rot, jnp.where(first, 0.0, s), 0.0)
    sb_ref[...] = jnp.where(first, -s, 0.0)


def _rope_tables(length, offset):
    tl = min(length, 1024)
    shp = jax.ShapeDtypeStruct((length, LANES), F32)
    spec = pl.BlockSpec((tl, LANES), lambda i: (i, 0))
    return pl.pallas_call(
        functools.partial(_rope_kernel, offset=offset),
        out_shape=(shp, shp, shp),
        grid=(length // tl,),
        out_specs=(spec, spec, spec),
        compiler_params=_params(("parallel",)),
        name="rope_tables",
    )()


def _rope_t_kernel(c_ref, s_ref, *, offset):
    n = c_ref.shape[1]
    fi = lax.broadcasted_iota(jnp.int32, (ROT_DIM // 2, n), 0).astype(F32)
    pos = (offset + pl.program_id(0) * n + lax.broadcasted_iota(jnp.int32, (ROT_DIM // 2, n), 1)).astype(F32)
    ang = _rope_angles(pos, fi)
    c_ref[...] = jnp.cos(ang)
    s_ref[...] = jnp.sin(ang)


def _rope_tables_t(length, offset):
    tn = min(length, 2048)
    shp = jax.ShapeDtypeStruct((ROT_DIM // 2, length), F32)
    spec = pl.BlockSpec((ROT_DIM // 2, tn), lambda i: (0, i))
    return pl.pallas_call(
        functools.partial(_rope_t_kernel, offset=offset),
        out_shape=(shp, shp),
        grid=(length // tn,),
        out_specs=(spec, spec),
        compiler_params=_params(("parallel",)),
        name="rope_tables_transposed",
    )()


Q_SCALE = HEAD_DIM ** -0.5 * math.log2(math.e)


def _modulated_norm(x, g, scale, shift):
    ms = jnp.mean(x * x, axis=-1, keepdims=True)
    y = x * lax.rsqrt(ms + EPS) * g
    return y * (1.0 + scale) + shift


def _inproj_kernel(*refs, k_t):
    if k_t:
        (x_ref, sc_ref, sh_ref, g_ref, w_ref, wk_ref, c_ref, sa_ref, sb_ref, ct_ref, st_ref,
         q_ref, k_ref, v_ref, u_ref, kb_ref, vb_ref) = refs
    else:
        (x_ref, sc_ref, sh_ref, g_ref, w_ref, wk_ref, c_ref, sa_ref, sb_ref,
         q_ref, k_ref, v_ref, u_ref, kb_ref, vb_ref) = refs
    nb, tl, d = x_ref.shape
    rows = nb * tl
    h = _modulated_norm(x_ref[...], g_ref[...], sc_ref[...], sh_ref[...])
    hb = h.reshape(rows, d).astype(BF16)
    cos = c_ref[...]
    sin_a = sa_ref[...]
    sin_b = sb_ref[...]

    def rope(t):
        ra = pltpu.roll(t, ROT_DIM // 2, 1).reshape(nb, tl, LANES)
        rb = pltpu.roll(t, LANES - ROT_DIM // 2, 1).reshape(nb, tl, LANES)
        return t.reshape(nb, tl, LANES) * cos + ra * sin_a + rb * sin_b

    q = jnp.dot(hb, w_ref[:, 0:ATTN_WIDTH], preferred_element_type=F32)
    for hd in range(N_HEADS):
        sl = slice(hd * LANES, (hd + 1) * LANES)
        q_ref[:, :, sl] = (rope(q[:, sl]) * Q_SCALE).astype(BF16)
    if k_t:
        kt = _nt_dot(wk_ref[...], hb)
        cos_t = ct_ref[...]
        sin_t = st_ref[...]
        half = ROT_DIM // 2
        for comp in range(2 * N_HEADS):
            b0 = comp * HEAD_DIM
            t1 = kt[b0:b0 + half]
            t2 = kt[b0 + half:b0 + ROT_DIM]
            blk = jnp.concatenate([t1 * cos_t - t2 * sin_t, t2 * cos_t + t1 * sin_t,
                                   kt[b0 + ROT_DIM:b0 + HEAD_DIM]], axis=0)
            k_ref[0, b0:b0 + HEAD_DIM, :] = blk
            kb_ref[0, b0:b0 + HEAD_DIM, :] = blk.astype(BF16)
    else:
        k = jnp.dot(hb, wk_ref[...], preferred_element_type=F32)
        for hd in range(N_HEADS):
            sl = slice(hd * LANES, (hd + 1) * LANES)
            kr = rope(k[:, sl])
            k_ref[:, :, sl] = kr
            kb_ref[:, :, sl] = kr.astype(BF16)
    v = jnp.dot(hb, w_ref[:, ATTN_WIDTH:2 * ATTN_WIDTH], preferred_element_type=F32)
    v = v.reshape(nb, tl, ATTN_WIDTH)
    v_ref[...] = v
    vb_ref[...] = v.astype(BF16)
    u = jnp.dot(hb, w_ref[:, 2 * ATTN_WIDTH:], preferred_element_type=F32)
    u_ref[...] = u.reshape(nb, tl, SSM_WIDTH)


def _inproj(x, scale, shift, g_mix, w_qvu, w_k, tables, tables_t, nb, tl):
    b, l, d = x.shape
    k_t = tables_t is not None
    assert not k_t or nb == 1
    grid = (b // nb, l // tl)
    xspec = lambda w: pl.BlockSpec((nb, tl, w), lambda i, j: (i, j, 0))
    mspec = pl.BlockSpec((nb, 1, d), lambda i, j: (i, 0, 0))
    tspec = pl.BlockSpec((tl, LANES), lambda i, j: (j, 0))
    const = lambda a: pl.BlockSpec(a.shape, lambda i, j: (0, 0), pipeline_mode=pl.Buffered(1))
    out = lambda dt: jax.ShapeDtypeStruct((b, l, ATTN_WIDTH), dt)
    in_specs = [xspec(d), mspec, mspec, pl.BlockSpec((1, d), lambda i, j: (0, 0)),
                const(w_qvu), const(w_k), tspec, tspec, tspec]
    args = [x, scale, shift, g_mix.reshape(1, d), w_qvu, w_k, *tables]
    kspec, kshape = xspec(ATTN_WIDTH), out
    if k_t:
        in_specs += [pl.BlockSpec((ROT_DIM // 2, tl), lambda i, j: (0, j))] * 2
        args += list(tables_t)
        kspec = pl.BlockSpec((1, ATTN_WIDTH, tl), lambda i, j: (i, 0, j))
        kshape = lambda dt: jax.ShapeDtypeStruct((b, ATTN_WIDTH, l), dt)
    return pl.pallas_call(
        functools.partial(_inproj_kernel, k_t=k_t),
        out_shape=(out(BF16), kshape(F32), out(F32), out(F32), kshape(BF16), out(BF16)),
        grid=grid,
        in_specs=in_specs,
        out_specs=(xspec(ATTN_WIDTH), kspec, xspec(ATTN_WIDTH), xspec(ATTN_WIDTH), kspec, xspec(ATTN_WIDTH)),
        compiler_params=_params(("parallel", "parallel")),
        name="norm_inproj_rope",
    )(*args)


ATTN_TQ = 512
ATTN_TK = 512
ATTN_NSUB = 2


def _stack_components(q):
    lane = lax.broadcasted_iota(jnp.int32, q.shape, 1)
    zero = jnp.zeros_like(q)
    return jnp.concatenate([jnp.where(lane < HEAD_DIM, q, zero),
                            jnp.where(lane >= HEAD_DIM, q, zero)], axis=0)


def _diff_epilogue(num, den, tq, lq1, lk1, lq2, lk2, g_subln):
    lam = (jnp.exp(jnp.sum(lq1 * lk1, axis=-1, keepdims=True))
           - jnp.exp(jnp.sum(lq2 * lk2, axis=-1, keepdims=True)) + LAM_INIT)
    on = num / den
    o = on[:tq] - lam * on[tq:]
    ms = jnp.mean(o * o, axis=-1, keepdims=True)
    return o * lax.rsqrt(ms + EPS) * g_subln * (1.0 - LAM_INIT)


def _nt_dot(a, b):
    return lax.dot_general(a, b, (((1,), (1,)), ((), ())), preferred_element_type=F32)


def _attn_prompt_kernel(lq1_ref, lk1_ref, lq2_ref, lk2_ref, gs_ref, q_ref, kt_ref, v_ref, o_ref,
                        qs_scr, s_scr, p_scr, m_scr, acc_scr, *, tq, tk, nsub):
    i = pl.program_id(2)
    rows = 2 * tq
    qs_scr[...] = _stack_components(q_ref[0])
    span = nsub * tk
    nfull = (i * tq) // span
    ones = jnp.ones((tk, LANES), BF16)
    m_scr[...] = jnp.full((rows, LANES), NEG_INF, F32)
    acc_scr[...] = jnp.zeros((rows, 2 * LANES), F32)

    def scores(start, u, masked):
        s = jnp.dot(qs_scr[...], kt_ref[0, :, pl.ds(start, tk)], preferred_element_type=F32)
        if masked:
            r = lax.broadcasted_iota(jnp.int32, s.shape, 0)
            c = lax.broadcasted_iota(jnp.int32, s.shape, 1)
            qc = (i * tq + jnp.where(r >= tq, r - tq, r)) // CHUNK
            s = jnp.where((start + c) // CHUNK <= qc, s, NEG_INF)
        s_scr[u] = s
        return jnp.broadcast_to(jnp.max(s, axis=-1, keepdims=True), (rows, LANES))

    def accumulate(start, u, block_max):
        m_old = m_scr[...]
        m_new = jnp.maximum(m_old, block_max)
        alpha = jnp.exp2(m_old - m_new)
        m_scr[...] = m_new
        p = jnp.exp2(s_scr[u] - jnp.concatenate([m_new] * (tk // LANES), axis=1))
        p_scr[u] = p.astype(BF16)
        v1 = jnp.concatenate([v_ref[0, pl.ds(start, tk), :], ones], axis=1)
        pv = jnp.dot(p_scr[u], v1, preferred_element_type=F32)
        acc_scr[...] = jnp.concatenate([alpha, alpha], axis=1) * acc_scr[...] + pv

    def body(t, _):
        base = pl.multiple_of(t * span, span)
        maxes = [scores(base + u * tk, u, False) for u in range(nsub)]
        for u in range(nsub):
            accumulate(base + u * tk, u, maxes[u])
        return 0
    lax.fori_loop(0, nfull, body, 0)

    def tail(j, _):
        start = pl.multiple_of(j * tk, tk)
        accumulate(start, 0, scores(start, 0, True))
        return 0
    lax.fori_loop(nfull * nsub, ((i + 1) * tq + tk - 1) // tk, tail, 0)

    acc = acc_scr[...]
    o = _diff_epilogue(acc[:, :LANES], acc[:, LANES:], tq, lq1_ref[...], lk1_ref[...], lq2_ref[...],
                       lk2_ref[...], gs_ref[...])
    o_ref[0] = o.astype(BF16)


def _attn_prompt(q, ktb, vb, lam_vecs, g_subln):
    b, l, _ = q.shape
    tq = min(l, ATTN_TQ)
    tk = min(l, ATTN_TK)
    nsub = ATTN_NSUB
    assert l % tq == 0 and l % tk == 0 and (nsub * tk) % tq == 0 and tq % CHUNK == 0 and tk % CHUNK == 0
    vec = pl.BlockSpec((1, HEAD_DIM), lambda bi, h, i: (0, 0))
    return pl.pallas_call(
        functools.partial(_attn_prompt_kernel, tq=tq, tk=tk, nsub=nsub),
        out_shape=jax.ShapeDtypeStruct((b, l, ATTN_WIDTH), BF16),
        grid=(b, N_HEADS, l // tq),
        in_specs=[vec, vec, vec, vec,
                  pl.BlockSpec((1, V_DIM), lambda bi, h, i: (0, 0)),
                  pl.BlockSpec((1, tq, LANES), lambda bi, h, i: (bi, i, h)),
                  pl.BlockSpec((1, LANES, l), lambda bi, h, i: (bi, h, 0)),
                  pl.BlockSpec((1, l, LANES), lambda bi, h, i: (bi, 0, h))],
        out_specs=pl.BlockSpec((1, tq, LANES), lambda bi, h, i: (bi, i, h)),
        scratch_shapes=[pltpu.VMEM((2 * tq, LANES), BF16), pltpu.VMEM((nsub, 2 * tq, tk), F32),
                        pltpu.VMEM((nsub, 2 * tq, tk), BF16),
                        pltpu.VMEM((2 * tq, LANES), F32), pltpu.VMEM((2 * tq, 2 * LANES), F32)],
        compiler_params=_params(("parallel", "parallel", "arbitrary")),
        name="prompt_diff_attention",
    )(*lam_vecs, g_subln.reshape(1, V_DIM), q, ktb, vb)


def _attn_sample_kernel(lq1_ref, lk1_ref, lq2_ref, lk2_ref, gs_ref, q_ref, ckt_ref, cv_ref,
                        kn_ref, vn_ref, o_ref):
    hd = pl.program_id(1)
    tq = q_ref.shape[1]
    past = ckt_ref.shape[2]
    qs = _stack_components(q_ref[0])
    s_p = jnp.dot(qs, ckt_ref[0].astype(BF16), preferred_element_type=F32)
    s_n = _nt_dot(qs, kn_ref[0])
    m = jnp.maximum(jnp.max(s_p, axis=-1, keepdims=True), jnp.max(s_n, axis=-1, keepdims=True))
    p_p = jnp.exp2(s_p - m)
    p_n = jnp.exp2(s_n - m)
    l = jnp.sum(p_p, axis=-1, keepdims=True) + jnp.sum(p_n, axis=-1, keepdims=True)
    vp = cv_ref[0, pl.ds(hd, past, stride=N_HEADS), :].astype(BF16)
    acc = (jnp.dot(p_p.astype(BF16), vp, preferred_element_type=F32)
           + jnp.dot(p_n.astype(BF16), vn_ref[0], preferred_element_type=F32))
    o = _diff_epilogue(acc, l, tq, lq1_ref[...], lk1_ref[...], lq2_ref[...], lk2_ref[...], gs_ref[...])
    o_ref[0] = o.astype(BF16)


def _attn_sample(q, cache_kt, cache_v2, kb, vb, lam_vecs, g_subln):
    b, l, _ = q.shape
    past = cache_kt.shape[2]
    vec = pl.BlockSpec((1, HEAD_DIM), lambda bi, h: (0, 0))
    new = pl.BlockSpec((1, l, LANES), lambda bi, h: (bi, 0, h))
    return pl.pallas_call(
        _attn_sample_kernel,
        out_shape=jax.ShapeDtypeStruct((b, l, ATTN_WIDTH), BF16),
        grid=(b, N_HEADS),
        in_specs=[vec, vec, vec, vec, pl.BlockSpec((1, V_DIM), lambda bi, h: (0, 0)),
                  new,
                  pl.BlockSpec((1, LANES, past), lambda bi, h: (bi, h, 0)),
                  pl.BlockSpec((1, past * N_HEADS, LANES), lambda bi, h: (bi, 0, 0)),
                  new, new],
        out_specs=new,
        compiler_params=_params(("parallel", "arbitrary")),
        name="sample_diff_attention",
    )(*lam_vecs, g_subln.reshape(1, V_DIM), q, cache_kt, cache_v2, kb, vb)


def _s5_discretize(lr, li, dt):
    mag = jnp.exp(lr * dt)
    ar = mag * jnp.cos(li * dt)
    ai = mag * jnp.sin(li * dt)
    den = lr * lr + li * li
    f_re = ((ar - 1.0) * lr + ai * li) / den
    f_im = (ai * lr - (ar - 1.0) * li) / den
    return f_re, f_im


def _s5_power(lr, li, dt, e):
    mag = jnp.exp(lr * dt * e)
    ang = li * dt * e
    return mag * jnp.cos(ang), mag * jnp.sin(ang)


def _s5_prep_kernel(lrr_ref, lir_ref, lrc_ref, lic_ref, ldt_ref, brt_ref, bit_ref, crt_ref, cit_ref,
                    mt_ref, w_ref, v_ref, a_ref):
    p = SSM_STATE
    lrr, lir = lrr_ref[0], lir_ref[0]
    lrc, lic = lrc_ref[0], lic_ref[0]
    dt = jnp.exp(ldt_ref[0])
    brt, bit = brt_ref[0], bit_ref[0]
    crt, cit = crt_ref[0], cit_ref[0]

    f_re, f_im = _s5_discretize(lrr, lir, dt)
    bbt_re = f_re * brt - f_im * bit
    bbt_im = f_re * bit + f_im * brt

    t_re = jnp.concatenate([bbt_re] * S5_CHUNK, axis=0)
    t_im = jnp.concatenate([bbt_im] * S5_CHUNK, axis=0)
    s_idx = lax.broadcasted_iota(jnp.int32, (S5_N, 1), 0) // SSM_GROUP
    pr, pi = _s5_power(lrr, lir, dt, (S5_CHUNK - 1 - s_idx).astype(F32))
    w_ref[0] = jnp.concatenate([pr * t_re - pi * t_im, pr * t_im + pi * t_re], axis=1)

    hh = lax.broadcasted_iota(jnp.int32, (SSM_GROUP, S5_N), 0)
    ll = lax.broadcasted_iota(jnp.int32, (SSM_GROUP, S5_N), 1)
    expand = (ll % SSM_GROUP == hh).astype(F32)
    ct_re = jnp.dot(crt, expand, precision=HIGHEST, preferred_element_type=F32)
    ct_im = jnp.dot(cit, expand, precision=HIGHEST, preferred_element_type=F32)
    t_idx = (lax.broadcasted_iota(jnp.int32, (1, S5_N), 1) // SSM_GROUP).astype(F32)

    pr, pi = _s5_power(lrc, lic, dt, t_idx + 1.0)
    v_ref[0, :p, :] = ct_re * pr - ct_im * pi
    v_ref[0, p:, :] = -(ct_re * pi + ct_im * pr)

    pr, pi = _s5_power(lrc, lic, dt, t_idx)
    vj_re = ct_re * pr - ct_im * pi
    vj_im = ct_re * pi + ct_im * pr
    kf = (jnp.dot(bbt_re, vj_re, precision=HIGHEST, preferred_element_type=F32)
          - jnp.dot(bbt_im, vj_im, precision=HIGHEST, preferred_element_type=F32))
    lane = lax.broadcasted_iota(jnp.int32, (SSM_GROUP, S5_N), 1)
    for s in range(S5_CHUNK):
        blk = kf if s == 0 else pltpu.roll(kf, SSM_GROUP * s, 1)
        blk = jnp.where(lane >= SSM_GROUP * s, blk, 0.0)
        mt_ref[0, s * SSM_GROUP:(s + 1) * SSM_GROUP, :] = blk

    ar, ai = _s5_power(lrr, lir, dt, float(S5_CHUNK))
    a_ref[0] = jnp.concatenate([ar, ai], axis=1)


def _s5_prep(lam_re, lam_im, log_dt, b_re, b_im, c_re, c_im):
    g, p = lam_re.shape
    row = pl.BlockSpec((1, 1, p), lambda i: (i, 0, 0))
    col = pl.BlockSpec((1, p, 1), lambda i: (i, 0, 0))
    bt = pl.BlockSpec((1, SSM_GROUP, p), lambda i: (i, 0, 0))
    ct = pl.BlockSpec((1, p, SSM_GROUP), lambda i: (i, 0, 0))
    return pl.pallas_call(
        _s5_prep_kernel,
        out_shape=(jax.ShapeDtypeStruct((g, S5_N, S5_N), F32),
                   jax.ShapeDtypeStruct((g, S5_N, 2 * p), F32),
                   jax.ShapeDtypeStruct((g, 2 * p, S5_N), F32),
                   jax.ShapeDtypeStruct((g, 1, 2 * p), F32)),
        grid=(g,),
        in_specs=[row, row, col, col, pl.BlockSpec((1, 1, 1), lambda i: (i, 0, 0)), bt, bt, ct, ct],
        out_specs=(pl.BlockSpec((1, S5_N, S5_N), lambda i: (i, 0, 0)),
                   pl.BlockSpec((1, S5_N, 2 * p), lambda i: (i, 0, 0)),
                   pl.BlockSpec((1, 2 * p, S5_N), lambda i: (i, 0, 0)),
                   pl.BlockSpec((1, 1, 2 * p), lambda i: (i, 0, 0))),
        compiler_params=_params(("parallel",)),
        name="s5_chunk_operators",
    )(lam_re.reshape(g, 1, p), lam_im.reshape(g, 1, p), lam_re.reshape(g, p, 1), lam_im.reshape(g, p, 1),
      log_dt.reshape(g, 1, 1), jnp.swapaxes(b_re, 1, 2), jnp.swapaxes(b_im, 1, 2),
      jnp.swapaxes(c_re, 1, 2), jnp.swapaxes(c_im, 1, 2))


def _s5_local_kernel(x_ref, w_ref, d_ref):
    d_ref[0] = jnp.dot(x_ref[0], w_ref[0].astype(BF16), preferred_element_type=F32)


def _s5_local(x, w):
    g, r, n = x.shape
    return pl.pallas_call(
        _s5_local_kernel,
        out_shape=jax.ShapeDtypeStruct((g, r, 2 * SSM_STATE), F32),
        grid=(g,),
        in_specs=[pl.BlockSpec((1, r, n), lambda i: (i, 0, 0)),
                  pl.BlockSpec((1, n, 2 * SSM_STATE), lambda i: (i, 0, 0))],
        out_specs=pl.BlockSpec((1, r, 2 * SSM_STATE), lambda i: (i, 0, 0)),
        compiler_params=_params(("parallel",)),
        name="s5_chunk_state_increment",
    )(x, w)


def _s5_scan_kernel(d_ref, h0_ref, a_ref, hs_ref, hf_ref, carry_ref):
    @pl.when(pl.program_id(0) == 0)
    def _():
        carry_ref[...] = h0_ref[...]

    a = a_ref[...]
    a_sw = pltpu.roll(a, SSM_STATE, 1)
    lane = lax.broadcasted_iota(jnp.int32, a.shape, 1)
    a_rr = jnp.where(lane < SSM_STATE, a, a_sw)
    a_ii = jnp.where(lane < SSM_STATE, -a_sw, a)

    def body(c, h):
        hs_ref[c] = h
        return h * a_rr + pltpu.roll(h, SSM_STATE, 1) * a_ii + d_ref[c]

    h = lax.fori_loop(0, d_ref.shape[0], body, carry_ref[...])
    carry_ref[...] = h
    hf_ref[...] = h


def _s5_scan(d, h0, a_rows, tc):
    nc, r2, w = d.shape
    return pl.pallas_call(
        _s5_scan_kernel,
        out_shape=(jax.ShapeDtypeStruct((nc, r2, w), F32), jax.ShapeDtypeStruct((r2, w), F32)),
        grid=(nc // tc,),
        in_specs=[pl.BlockSpec((tc, r2, w), lambda i: (i, 0, 0)),
                  pl.BlockSpec((r2, w), lambda i: (0, 0)),
                  pl.BlockSpec((r2, w), lambda i: (0, 0))],
        out_specs=(pl.BlockSpec((tc, r2, w), lambda i: (i, 0, 0)),
                   pl.BlockSpec((r2, w), lambda i: (0, 0))),
        scratch_shapes=[pltpu.VMEM((r2, w), F32)],
        compiler_params=_params(("arbitrary",)),
        name="s5_chunk_state_scan",
    )(d, h0, a_rows)


def _s5_out_kernel(x_ref, hs_ref, mt_ref, v_ref, y_ref):
    y_ref[0] = (jnp.dot(x_ref[0], mt_ref[0].astype(BF16), preferred_element_type=F32)
                + jnp.dot(hs_ref[0], v_ref[0].astype(BF16), preferred_element_type=F32))


def _s5_out(x, hs, mt, v):
    g, r, n = x.shape
    return pl.pallas_call(
        _s5_out_kernel,
        out_shape=jax.ShapeDtypeStruct((g, r, n), F32),
        grid=(g,),
        in_specs=[pl.BlockSpec((1, r, n), lambda i: (i, 0, 0)),
                  pl.BlockSpec((1, r, 2 * SSM_STATE), lambda i: (i, 0, 0)),
                  pl.BlockSpec((1, n, n), lambda i: (i, 0, 0)),
                  pl.BlockSpec((1, 2 * SSM_STATE, n), lambda i: (i, 0, 0))],
        out_specs=pl.BlockSpec((1, r, n), lambda i: (i, 0, 0)),
        compiler_params=_params(("parallel",)),
        name="s5_chunk_output",
    )(x, hs, mt, v)


def _s5(u, h0_re, h0_im, ops):
    mt, w, v, a = ops
    b, l, _ = u.shape
    g, p = N_GROUPS, SSM_STATE
    nc = l // S5_CHUNK
    x = u.reshape(b, nc, S5_CHUNK, g, SSM_GROUP).transpose(3, 0, 1, 2, 4)
    x = x.reshape(g, b * nc, S5_N).astype(BF16)
    d = _s5_local(x, w)
    d = d.reshape(g, b, nc, 2 * p).transpose(2, 1, 0, 3).reshape(nc, b * g, 2 * p)
    if h0_re is None:
        h0 = jnp.zeros((b * g, 2 * p), F32)
    else:
        h0 = jnp.concatenate([h0_re, h0_im], axis=-1).astype(F32).reshape(b * g, 2 * p)
    a_rows = jnp.broadcast_to(a.reshape(1, g, 2 * p), (b, g, 2 * p)).reshape(b * g, 2 * p)
    hs, hf = _s5_scan(d, h0, a_rows, tc=min(nc, 32))
    hs = hs.reshape(nc, b, g, 2 * p).transpose(2, 1, 0, 3).reshape(g, b * nc, 2 * p).astype(BF16)
    y = _s5_out(x, hs, mt, v)
    y = y.reshape(g, b, nc, S5_CHUNK, SSM_GROUP).transpose(1, 2, 3, 0, 4).reshape(b, l, SSM_WIDTH)
    hf = hf.reshape(b, g, 2 * p)
    return y, hf[..., :p], hf[..., p:]


def _gelu_tanh(x):
    return x * (0.5 * (1.0 + jnp.tanh(math.sqrt(2.0 / math.pi) * (x + 0.044715 * (x * x * x)))))


def _mix_kernel(x_ref, y_ref, u_ref, o_ref, gate_ref, d_ref, wg_ref, bg_ref, wo_ref, out_ref):
    nb, tl, d = x_ref.shape
    rows = nb * tl
    y = y_ref[...].reshape(rows, SSM_WIDTH) + d_ref[...] * u_ref[...].reshape(rows, SSM_WIDTH)
    z = _gelu_tanh(y).astype(BF16)
    gl = jnp.dot(z, wg_ref[...], preferred_element_type=F32) + bg_ref[...]
    yg = (gl[:, :SSM_WIDTH] * _sigmoid(gl[:, SSM_WIDTH:])).astype(BF16)
    mix = (jnp.dot(o_ref[...].reshape(rows, ATTN_WIDTH), wo_ref[:ATTN_WIDTH, :], preferred_element_type=F32)
           + jnp.dot(yg, wo_ref[ATTN_WIDTH:, :], preferred_element_type=F32))
    out_ref[...] = x_ref[...] + gate_ref[...] * mix.reshape(nb, tl, d)


def _mix(x, y_ssm, u, o_attn, gate, ssm_d, w_glu_b, b_glu, w_out_b, nb, tl):
    b, l, d = x.shape
    xspec = lambda w: pl.BlockSpec((nb, tl, w), lambda i, j: (i, j, 0))
    const = lambda shape: pl.BlockSpec(shape, lambda i, j: (0, 0), pipeline_mode=pl.Buffered(1))
    return pl.pallas_call(
        _mix_kernel,
        out_shape=jax.ShapeDtypeStruct((b, l, d), F32),
        grid=(b // nb, l // tl),
        in_specs=[xspec(d), xspec(SSM_WIDTH), xspec(SSM_WIDTH), xspec(ATTN_WIDTH),
                  pl.BlockSpec((nb, 1, d), lambda i, j: (i, 0, 0)),
                  const((1, SSM_WIDTH)), const((SSM_WIDTH, 2 * SSM_WIDTH)), const((1, 2 * SSM_WIDTH)),
                  const((d, d))],
        out_specs=xspec(d),
        compiler_params=_params(("parallel", "parallel")),
        name="glu_outproj_residual",
    )(x, y_ssm, u, o_attn, gate, ssm_d.reshape(1, SSM_WIDTH), w_glu_b, b_glu.reshape(1, -1), w_out_b)


def _ffn_kernel(x_ref, sc_ref, sh_ref, gate_ref, g_ref, gf_ref, w1_ref, w2_ref, y_ref, h_ref, acc_ref):
    nb, tl, d = x_ref.shape
    j = pl.program_id(2)

    @pl.when(j == 0)
    def _():
        h = _modulated_norm(x_ref[...], g_ref[...], sc_ref[...], sh_ref[...])
        h_ref[...] = h.reshape(nb * tl, d).astype(BF16)
        acc_ref[...] = jnp.zeros_like(acc_ref)

    hid = jnp.dot(h_ref[...], w1_ref[...], preferred_element_type=F32)
    hid = jnp.square(jnp.maximum(hid, 0.0)).astype(BF16)
    acc_ref[...] += jnp.dot(hid, w2_ref[...], preferred_element_type=F32)

    @pl.when(j == pl.num_programs(2) - 1)
    def _():
        x = x_ref[...] + gate_ref[...] * acc_ref[...].reshape(nb, tl, d)
        ms = jnp.mean(x * x, axis=-1, keepdims=True)
        y_ref[...] = x * lax.rsqrt(ms + EPS) * gf_ref[...]


def _ffn(x, scale, shift, gate, g_ffn, g_final, w1_b, w2_b, nb, tl, tf):
    b, l, d = x.shape
    xspec = pl.BlockSpec((nb, tl, d), lambda i, j, k: (i, j, 0))
    mspec = pl.BlockSpec((nb, 1, d), lambda i, j, k: (i, 0, 0))
    gspec = pl.BlockSpec((1, d), lambda i, j, k: (0, 0))
    return pl.pallas_call(
        _ffn_kernel,
        out_shape=jax.ShapeDtypeStruct((b, l, d), F32),
        grid=(b // nb, l // tl, D_FF // tf),
        in_specs=[xspec, mspec, mspec, mspec, gspec, gspec,
                  pl.BlockSpec((d, tf), lambda i, j, k: (0, k)),
                  pl.BlockSpec((tf, d), lambda i, j, k: (k, 0))],
        out_specs=xspec,
        scratch_shapes=[pltpu.VMEM((nb * tl, d), BF16), pltpu.VMEM((nb * tl, d), F32)],
        compiler_params=_params(("parallel", "parallel", "arbitrary")),
        name="relu2_mlp_final_norm",
    )(x, scale, shift, gate, g_ffn.reshape(1, d), g_final.reshape(1, d), w1_b, w2_b)


def _layer(x, mods, tables, tables_t, past_kt, past_v2, h0_re, h0_im, wts, nb, tl):
    shift1, scale1, gate1, shift2, scale2, gate2 = mods
    b, l, _ = x.shape
    first_chunk = past_kt is None
    w_k = wts["w_k_t"] if first_chunk else wts["w_k"]
    q, k, v, u, kb, vb = _inproj(x, scale1, shift1, wts["g_mix"], wts["w_qvu"], w_k, tables, tables_t, nb, tl)
    if first_chunk:
        o = _attn_prompt(q, kb, vb, wts["lam_vecs"], wts["g_subln"])
        k6 = k.reshape(b, N_HEADS, 2, HEAD_DIM, l).transpose(0, 4, 1, 2, 3)[None]
    else:
        o = _attn_sample(q, past_kt, past_v2, kb, vb, wts["lam_vecs"], wts["g_subln"])
        k6 = k.reshape(1, b, l, N_HEADS, 2, HEAD_DIM)
    y_ssm, h_re, h_im = _s5(u, h0_re, h0_im, wts["s5_ops"])
    x1 = _mix(x, y_ssm, u, o, gate1, wts["ssm_d"], wts["w_glu"], wts["b_glu"], wts["w_out"], nb, tl)
    y = _ffn(x1, scale2, shift2, gate2, wts["g_ffn"], wts["g_final"], wts["w_ff1"], wts["w_ff2"],
             nb, tl, tf=512)
    v5 = v.reshape(1, b, l, N_HEADS, V_DIM)
    return y, k6, v5, h_re[None], h_im[None]


def kernel(x_prompt, x_sample, c_prompt, c_sample, cache_k, cache_v, state_ssm_re, state_ssm_im, w_ada, b_ada, g_mix, w_in, lam_q1, lam_k1, lam_q2, lam_k2, g_subln, ssm_lam_re, ssm_lam_im, ssm_log_dt, ssm_b_re, ssm_b_im, ssm_c_re, ssm_c_im, ssm_d, w_glu, b_glu, w_out, g_ffn, w_ff1, w_ff2, g_final):
    bp, lp, d = x_prompt.shape
    bs, ls, _ = x_sample.shape
    past = cache_k.shape[2]

    m = _ada(jnp.concatenate([c_prompt, c_sample], axis=0), w_ada[0], b_ada[0])
    mods = [m[:, i * d:(i + 1) * d].reshape(bp + bs, 1, d) for i in range(6)]
    mods_p = [t[:bp] for t in mods]
    mods_s = [t[bp:] for t in mods]

    w_in_b = w_in[0].astype(BF16)
    w_k = w_in_b[:, ATTN_WIDTH:2 * ATTN_WIDTH]
    wts = {
        "g_mix": g_mix[0],
        "w_qvu": jnp.concatenate([w_in_b[:, :ATTN_WIDTH], w_in_b[:, 2 * ATTN_WIDTH:]], axis=1),
        "w_k": w_k, "w_k_t": w_k.T,
        "lam_vecs": tuple(t[0].reshape(1, HEAD_DIM) for t in (lam_q1, lam_k1, lam_q2, lam_k2)),
        "g_subln": g_subln[0],
        "s5_ops": _s5_prep(ssm_lam_re[0], ssm_lam_im[0], ssm_log_dt[0], ssm_b_re[0], ssm_b_im[0],
                           ssm_c_re[0], ssm_c_im[0]),
        "ssm_d": ssm_d[0], "w_glu": w_glu[0].astype(BF16), "b_glu": b_glu[0],
        "w_out": w_out[0].astype(BF16), "g_ffn": g_ffn[0], "g_final": g_final,
        "w_ff1": w_ff1[0].astype(BF16), "w_ff2": w_ff2[0].astype(BF16),
    }

    tl_p = min(lp, 512)
    yp, kp, vp, srp, sip = _layer(x_prompt, mods_p, _rope_tables(lp, 0), _rope_tables_t(lp, 0),
                                  None, None, None, None, wts, nb=1, tl=tl_p)
    nb_s = max(1, min(bs, 512 // ls))
    ckt = cache_k[0].reshape(bs, past, ATTN_WIDTH).transpose(0, 2, 1)
    cv2 = cache_v[0].reshape(bs, past * N_HEADS, V_DIM)
    ys, ks, vs, srs, sis = _layer(x_sample, mods_s, _rope_tables(ls, past), None, ckt, cv2,
                                  state_ssm_re[0], state_ssm_im[0], wts, nb=nb_s, tl=ls)
    return (yp, ys, kp, vp, srp, sip, ks, vs, srs, sis)
```

```python
import functools
import math

import jax
import jax.numpy as jnp
from jax import lax
from jax.experimental import pallas as pl
from jax.experimental.pallas import tpu as pltpu

F32 = jnp.float32
BF16 = jnp.bfloat16

D_MODEL = 2048
ATTN_WIDTH = D_MODEL // 2
SSM_WIDTH = D_MODEL - ATTN_WIDTH
N_HEADS = 8
HEAD_DIM = ATTN_WIDTH // (2 * N_HEADS)
V_DIM = 2 * HEAD_DIM
ROT_DIM = HEAD_DIM // 4
ROPE_THETA = 500000.0
CHUNK = 64
SSM_GROUP = 16
N_GROUPS = SSM_WIDTH // SSM_GROUP
SSM_STATE = 64
IN_WIDTH = 3 * ATTN_WIDTH + SSM_WIDTH
D_FF = 4 * D_MODEL
EPS = 1e-6
NEG_INF = -1e30
LAM_INIT = 0.8 - 0.6 * math.exp(-0.3 * 0)

LANES = 128
S5_CHUNK = 16
S5_N = S5_CHUNK * SSM_GROUP
S5_BLOCKS = SSM_WIDTH // LANES
S5_GPB = LANES // SSM_GROUP
S5_K = S5_CHUNK * LANES
S5_HALF = S5_GPB * SSM_STATE
VMEM_LIMIT = 56 * 1024 * 1024

HIGHEST = lax.Precision.HIGHEST


def _params(sem, vmem=VMEM_LIMIT):
    return pltpu.CompilerParams(dimension_semantics=sem, vmem_limit_bytes=vmem)


def _sigmoid(x):
    return 1.0 / (1.0 + jnp.exp(-x))


def _ada_kernel(c_ref, w_ref, b_ref, o_ref):
    c = c_ref[...]
    s = (c * _sigmoid(c)).astype(BF16)
    o_ref[...] = jnp.dot(s, w_ref[...].astype(BF16), preferred_element_type=F32) + b_ref[...]


def _ada(c_all, w_ada, b_ada):
    nb, d = c_all.shape
    n = w_ada.shape[1]
    tn = 1024
    return pl.pallas_call(
        _ada_kernel,
        out_shape=jax.ShapeDtypeStruct((nb, n), F32),
        grid=(n // tn,),
        in_specs=[pl.BlockSpec((nb, d), lambda j: (0, 0)),
                  pl.BlockSpec((d, tn), lambda j: (0, j)),
                  pl.BlockSpec((1, tn), lambda j: (0, j))],
        out_specs=pl.BlockSpec((nb, tn), lambda j: (0, j)),
        compiler_params=_params(("parallel",)),
        name="ada_modulation",
    )(c_all, w_ada, b_ada.reshape(1, n))


def _rope_angles(pos, freq_index):
    inv = jnp.exp(freq_index * (-2.0 * math.log(ROPE_THETA) / ROT_DIM))
    return pos * inv


def _rope_kernel(c_ref, sa_ref, sb_ref, *, offset):
    tl = c_ref.shape[0]
    row = lax.broadcasted_iota(jnp.int32, (tl, LANES), 0)
    lane = lax.broadcasted_iota(jnp.int32, (tl, LANES), 1)
    pos = (offset + pl.program_id(0) * tl + row).astype(F32)
    d = lane % HEAD_DIM
    ang = _rope_angles(pos, (d % (ROT_DIM // 2)).astype(F32))
    c = jnp.cos(ang)
    s = jnp.sin(ang)
    in_rot = d < ROT_DIM
    first = d < ROT_DIM // 2
    c_ref[...] = jnp.where(in_rot, c, 1.0)
    sa_ref[...] = jnp.where(in_rot, jnp.where(first, 0.0, s), 0.0)
    sb_ref[...] = jnp.where(first, -s, 0.0)


def _rope_tables(length, offset):
    tl = min(length, 1024)
    shp = jax.ShapeDtypeStruct((length, LANES), F32)
    spec = pl.BlockSpec((tl, LANES), lambda i: (i, 0))
    return pl.pallas_call(
        functools.partial(_rope_kernel, offset=offset),
        out_shape=(shp, shp, shp),
        grid=(length // tl,),
        out_specs=(spec, spec, spec),
        compiler_params=_params(("parallel",)),
        name="rope_tables",
    )()


def _rope_t_kernel(c_ref, s_ref, *, offset):
    n = c_ref.shape[1]
    fi = lax.broadcasted_iota(jnp.int32, (ROT_DIM // 2, n), 0).astype(F32)
    pos = (offset + pl.program_id(0) * n + lax.broadcasted_iota(jnp.int32, (ROT_DIM // 2, n), 1)).astype(F32)
    ang = _rope_angles(pos, fi)
    c_ref[...] = jnp.cos(ang)
    s_ref[...] = jnp.sin(ang)


def _rope_tables_t(length, offset):
    tn = min(length, 2048)
    shp = jax.ShapeDtypeStruct((ROT_DIM // 2, length), F32)
    spec = pl.BlockSpec((ROT_DIM // 2, tn), lambda i: (0, i))
    return pl.pallas_call(
        functools.partial(_rope_t_kernel, offset=offset),
        out_shape=(shp, shp),
        grid=(length // tn,),
        out_specs=(spec, spec),
        compiler_params=_params(("parallel",)),
        name="rope_tables_transposed",
    )()


Q_SCALE = HEAD_DIM ** -0.5 * math.log2(math.e)


def _modulated_norm(x, g, scale, shift):
    ms = jnp.mean(x * x, axis=-1, keepdims=True)
    y = x * lax.rsqrt(ms + EPS) * g
    return y * (1.0 + scale) + shift


def _inproj_kernel(*refs, k_t):
    if k_t:
        (x_ref, sc_ref, sh_ref, g_ref, w_ref, wk_ref, c_ref, sa_ref, sb_ref, ct_ref, st_ref,
         q_ref, k_ref, v_ref, u_ref, kb_ref, vb_ref) = refs
    else:
        (x_ref, sc_ref, sh_ref, g_ref, w_ref, wk_ref, c_ref, sa_ref, sb_ref,
         q_ref, k_ref, v_ref, u_ref, kb_ref, vb_ref) = refs
    nb, tl, d = x_ref.shape
    rows = nb * tl
    h = _modulated_norm(x_ref[...], g_ref[...], sc_ref[...], sh_ref[...])
    hb = h.reshape(rows, d).astype(BF16)
    cos = c_ref[...]
    sin_a = sa_ref[...]
    sin_b = sb_ref[...]

    def rope(t):
        ra = pltpu.roll(t, ROT_DIM // 2, 1).reshape(nb, tl, LANES)
        rb = pltpu.roll(t, LANES - ROT_DIM // 2, 1).reshape(nb, tl, LANES)
        return t.reshape(nb, tl, LANES) * cos + ra * sin_a + rb * sin_b

    q = jnp.dot(hb, w_ref[:, 0:ATTN_WIDTH], preferred_element_type=F32)
    for hd in range(N_HEADS):
        sl = slice(hd * LANES, (hd + 1) * LANES)
        q_ref[:, :, sl] = (rope(q[:, sl]) * Q_SCALE).astype(BF16)
    if k_t:
        kt = _nt_dot(wk_ref[...], hb)
        cos_t = ct_ref[...]
        sin_t = st_ref[...]
        half = ROT_DIM // 2
        for comp in range(2 * N_HEADS):
            b0 = comp * HEAD_DIM
            t1 = kt[b0:b0 + half]
            t2 = kt[b0 + half:b0 + ROT_DIM]
            blk = jnp.concatenate([t1 * cos_t - t2 * sin_t, t2 * cos_t + t1 * sin_t,
                                   kt[b0 + ROT_DIM:b0 + HEAD_DIM]], axis=0)
            k_ref[0, b0:b0 + HEAD_DIM, :] = blk
            kb_ref[0, b0:b0 + HEAD_DIM, :] = blk.astype(BF16)
    else:
        k = jnp.dot(hb, wk_ref[...], preferred_element_type=F32)
        for hd in range(N_HEADS):
            sl = slice(hd * LANES, (hd + 1) * LANES)
            kr = rope(k[:, sl])
            k_ref[:, :, sl] = kr
            kb_ref[:, :, sl] = kr.astype(BF16)
    v = jnp.dot(hb, w_ref[:, ATTN_WIDTH:2 * ATTN_WIDTH], preferred_element_type=F32)
    v = v.reshape(nb, tl, ATTN_WIDTH)
    v_ref[...] = v
    vb_ref[...] = v.astype(BF16)
    u = jnp.dot(hb, w_ref[:, 2 * ATTN_WIDTH:], preferred_element_type=F32)
    u_ref[...] = u.reshape(nb, tl, SSM_WIDTH)


def _inproj(x, scale, shift, g_mix, w_qvu, w_k, tables, tables_t, nb, tl):
    b, l, d = x.shape
    k_t = tables_t is not None
    assert not k_t or nb == 1
    grid = (b // nb, l // tl)
    xspec = lambda w: pl.BlockSpec((nb, tl, w), lambda i, j: (i, j, 0))
    mspec = pl.BlockSpec((nb, 1, d), lambda i, j: (i, 0, 0))
    tspec = pl.BlockSpec((tl, LANES), lambda i, j: (j, 0))
    const = lambda a: pl.BlockSpec(a.shape, lambda i, j: (0, 0), pipeline_mode=pl.Buffered(1))
    out = lambda dt: jax.ShapeDtypeStruct((b, l, ATTN_WIDTH), dt)
    in_specs = [xspec(d), mspec, mspec, pl.BlockSpec((1, d), lambda i, j: (0, 0)),
                const(w_qvu), const(w_k), tspec, tspec, tspec]
    args = [x, scale, shift, g_mix.reshape(1, d), w_qvu, w_k, *tables]
    kspec, kshape = xspec(ATTN_WIDTH), out
    if k_t:
        in_specs += [pl.BlockSpec((ROT_DIM // 2, tl), lambda i, j: (0, j))] * 2
        args += list(tables_t)
        kspec = pl.BlockSpec((1, ATTN_WIDTH, tl), lambda i, j: (i, 0, j))
        kshape = lambda dt: jax.ShapeDtypeStruct((b, ATTN_WIDTH, l), dt)
    return pl.pallas_call(
        functools.partial(_inproj_kernel, k_t=k_t),
        out_shape=(out(BF16), kshape(F32), out(F32), out(F32), kshape(BF16), out(BF16)),
        grid=grid,
        in_specs=in_specs,
        out_specs=(xspec(ATTN_WIDTH), kspec, xspec(ATTN_WIDTH), xspec(ATTN_WIDTH), kspec, xspec(ATTN_WIDTH)),
        compiler_params=_params(("parallel", "parallel")),
        name="norm_inproj_rope",
    )(*args)


ATTN_TQ = 512
ATTN_TK = 512
ATTN_NSUB = 2


def _stack_components(q):
    lane = lax.broadcasted_iota(jnp.int32, q.shape, 1)
    zero = jnp.zeros_like(q)
    return jnp.concatenate([jnp.where(lane < HEAD_DIM, q, zero),
                            jnp.where(lane >= HEAD_DIM, q, zero)], axis=0)


def _diff_epilogue(num, den, tq, lq1, lk1, lq2, lk2, g_subln):
    lam = (jnp.exp(jnp.sum(lq1 * lk1, axis=-1, keepdims=True))
           - jnp.exp(jnp.sum(lq2 * lk2, axis=-1, keepdims=True)) + LAM_INIT)
    on = num / den
    o = on[:tq] - lam * on[tq:]
    ms = jnp.mean(o * o, axis=-1, keepdims=True)
    return o * lax.rsqrt(ms + EPS) * g_subln * (1.0 - LAM_INIT)


def _nt_dot(a, b):
    return lax.dot_general(a, b, (((1,), (1,)), ((), ())), preferred_element_type=F32)


def _attn_prompt_kernel(lq1_ref, lk1_ref, lq2_ref, lk2_ref, gs_ref, q_ref, kt_ref, v_ref, o_ref,
                        qs_scr, s_scr, p_scr, m_scr, acc_scr, *, tq, tk, nsub):
    i = pl.program_id(2)
    rows = 2 * tq
    qs_scr[...] = _stack_components(q_ref[0])
    span = nsub * tk
    nfull = (i * tq) // span
    ones = jnp.ones((tk, LANES), BF16)
    m_scr[...] = jnp.full((rows, LANES), NEG_INF, F32)
    acc_scr[...] = jnp.zeros((rows, 2 * LANES), F32)

    def scores(start, u, masked):
        s = jnp.dot(qs_scr[...], kt_ref[0, :, pl.ds(start, tk)], preferred_element_type=F32)
        if masked:
            r = lax.broadcasted_iota(jnp.int32, s.shape, 0)
            c = lax.broadcasted_iota(jnp.int32, s.shape, 1)
            qc = (i * tq + jnp.where(r >= tq, r - tq, r)) // CHUNK
            s = jnp.where((start + c) // CHUNK <= qc, s, NEG_INF)
        s_scr[u] = s
        return jnp.broadcast_to(jnp.max(s, axis=-1, keepdims=True), (rows, LANES))

    def accumulate(start, u, block_max):
        m_old = m_scr[...]
        m_new = jnp.maximum(m_old, block_max)
        alpha = jnp.exp2(m_old - m_new)
        m_scr[...] = m_new
        p = jnp.exp2(s_scr[u] - jnp.concatenate([m_new] * (tk // LANES), axis=1))
        p_scr[u] = p.astype(BF16)
        v1 = jnp.concatenate([v_ref[0, pl.ds(start, tk), :], ones], axis=1)
        pv = jnp.dot(p_scr[u], v1, preferred_element_type=F32)
        acc_scr[...] = jnp.concatenate([alpha, alpha], axis=1) * acc_scr[...] + pv

    def body(t, _):
        base = pl.multiple_of(t * span, span)
        maxes = [scores(base + u * tk, u, False) for u in range(nsub)]
        for u in range(nsub):
            accumulate(base + u * tk, u, maxes[u])
        return 0
    lax.fori_loop(0, nfull, body, 0)

    left = i - nfull * nsub
    for n_left in range(nsub):
        @pl.when(left == n_left)
        def _(n_left=n_left):
            base = pl.multiple_of(nfull * span, span)
            maxes = [scores(base + u * tk, u, u == n_left) for u in range(n_left + 1)]
            for u in range(n_left + 1):
                accumulate(base + u * tk, u, maxes[u])

    acc = acc_scr[...]
    o = _diff_epilogue(acc[:, :LANES], acc[:, LANES:], tq, lq1_ref[...], lk1_ref[...], lq2_ref[...],
                       lk2_ref[...], gs_ref[...])
    o_ref[0] = o.astype(BF16)


def _attn_prompt(q, ktb, vb, lam_vecs, g_subln):
    b, l, _ = q.shape
    tq = min(l, ATTN_TQ)
    tk = min(l, ATTN_TK)
    nsub = ATTN_NSUB
    assert l % tq == 0 and tq == tk and tq % CHUNK == 0
    vec = pl.BlockSpec((1, HEAD_DIM), lambda bi, h, i: (0, 0))
    return pl.pallas_call(
        functools.partial(_attn_prompt_kernel, tq=tq, tk=tk, nsub=nsub),
        out_shape=jax.ShapeDtypeStruct((b, l, ATTN_WIDTH), BF16),
        grid=(b, N_HEADS, l // tq),
        in_specs=[vec, vec, vec, vec,
                  pl.BlockSpec((1, V_DIM), lambda bi, h, i: (0, 0)),
                  pl.BlockSpec((1, tq, LANES), lambda bi, h, i: (bi, i, h)),
                  pl.BlockSpec((1, LANES, l), lambda bi, h, i: (bi, h, 0)),
                  pl.BlockSpec((1, l, LANES), lambda bi, h, i: (bi, 0, h))],
        out_specs=pl.BlockSpec((1, tq, LANES), lambda bi, h, i: (bi, i, h)),
        scratch_shapes=[pltpu.VMEM((2 * tq, LANES), BF16), pltpu.VMEM((nsub, 2 * tq, tk), F32),
                        pltpu.VMEM((nsub, 2 * tq, tk), BF16),
                        pltpu.VMEM((2 * tq, LANES), F32), pltpu.VMEM((2 * tq, 2 * LANES), F32)],
        compiler_params=_params(("parallel", "parallel", "arbitrary")),
        name="prompt_diff_attention",
    )(*lam_vecs, g_subln.reshape(1, V_DIM), q, ktb, vb)


def _attn_sample_kernel(lq1_ref, lk1_ref, lq2_ref, lk2_ref, gs_ref, q_ref, ckt_ref, cv_ref,
                        kn_ref, vn_ref, o_ref):
    hd = pl.program_id(1)
    tq = q_ref.shape[1]
    past = ckt_ref.shape[2]
    qs = _stack_components(q_ref[0])
    s_p = jnp.dot(qs, ckt_ref[0].astype(BF16), preferred_element_type=F32)
    s_n = _nt_dot(qs, kn_ref[0])
    m = jnp.maximum(jnp.max(s_p, axis=-1, keepdims=True), jnp.max(s_n, axis=-1, keepdims=True))
    p_p = jnp.exp2(s_p - m)
    p_n = jnp.exp2(s_n - m)
    l = jnp.sum(p_p, axis=-1, keepdims=True) + jnp.sum(p_n, axis=-1, keepdims=True)
    vp = cv_ref[0, pl.ds(hd, past, stride=N_HEADS), :].astype(BF16)
    acc = (jnp.dot(p_p.astype(BF16), vp, preferred_element_type=F32)
           + jnp.dot(p_n.astype(BF16), vn_ref[0], preferred_element_type=F32))
    o = _diff_epilogue(acc, l, tq, lq1_ref[...], lk1_ref[...], lq2_ref[...], lk2_ref[...], gs_ref[...])
    o_ref[0] = o.astype(BF16)


def _attn_sample(q, cache_kt, cache_v2, kb, vb, lam_vecs, g_subln):
    b, l, _ = q.shape
    past = cache_kt.shape[2]
    vec = pl.BlockSpec((1, HEAD_DIM), lambda bi, h: (0, 0))
    new = pl.BlockSpec((1, l, LANES), lambda bi, h: (bi, 0, h))
    return pl.pallas_call(
        _attn_sample_kernel,
        out_shape=jax.ShapeDtypeStruct((b, l, ATTN_WIDTH), BF16),
        grid=(b, N_HEADS),
        in_specs=[vec, vec, vec, vec, pl.BlockSpec((1, V_DIM), lambda bi, h: (0, 0)),
                  new,
                  pl.BlockSpec((1, LANES, past), lambda bi, h: (bi, h, 0)),
                  pl.BlockSpec((1, past * N_HEADS, LANES), lambda bi, h: (bi, 0, 0)),
                  new, new],
        out_specs=new,
        compiler_params=_params(("parallel", "arbitrary")),
        name="sample_diff_attention",
    )(*lam_vecs, g_subln.reshape(1, V_DIM), q, cache_kt, cache_v2, kb, vb)


def _s5_discretize(lr, li, dt):
    mag = jnp.exp(lr * dt)
    ar = mag * jnp.cos(li * dt)
    ai = mag * jnp.sin(li * dt)
    den = lr * lr + li * li
    f_re = ((ar - 1.0) * lr + ai * li) / den
    f_im = (ai * lr - (ar - 1.0) * li) / den
    return f_re, f_im


def _s5_power(lr, li, dt, e):
    mag = jnp.exp(lr * dt * e)
    ang = li * dt * e
    return mag * jnp.cos(ang), mag * jnp.sin(ang)


def _s5_prep_kernel(lrr_ref, lir_ref, lrc_ref, lic_ref, ldt_ref, brt_ref, bit_ref, crt_ref, cit_ref,
                    mt_ref, w_ref, v_ref, a_ref):
    p = SSM_STATE
    lrr, lir = lrr_ref[0], lir_ref[0]
    lrc, lic = lrc_ref[0], lic_ref[0]
    dt = jnp.exp(ldt_ref[0])
    brt, bit = brt_ref[0], bit_ref[0]
    crt, cit = crt_ref[0], cit_ref[0]

    f_re, f_im = _s5_discretize(lrr, lir, dt)
    bbt_re = f_re * brt - f_im * bit
    bbt_im = f_re * bit + f_im * brt

    t_re = jnp.concatenate([bbt_re] * S5_CHUNK, axis=0)
    t_im = jnp.concatenate([bbt_im] * S5_CHUNK, axis=0)
    s_idx = lax.broadcasted_iota(jnp.int32, (S5_N, 1), 0) // SSM_GROUP
    pr, pi = _s5_power(lrr, lir, dt, (S5_CHUNK - 1 - s_idx).astype(F32))
    w_ref[0] = jnp.concatenate([pr * t_re - pi * t_im, pr * t_im + pi * t_re], axis=1)

    hh = lax.broadcasted_iota(jnp.int32, (SSM_GROUP, S5_N), 0)
    ll = lax.broadcasted_iota(jnp.int32, (SSM_GROUP, S5_N), 1)
    expand = (ll % SSM_GROUP == hh).astype(F32)
    ct_re = jnp.dot(crt, expand, precision=HIGHEST, preferred_element_type=F32)
    ct_im = jnp.dot(cit, expand, precision=HIGHEST, preferred_element_type=F32)
    t_idx = (lax.broadcasted_iota(jnp.int32, (1, S5_N), 1) // SSM_GROUP).astype(F32)

    pr, pi = _s5_power(lrc, lic, dt, t_idx + 1.0)
    v_ref[0, :p, :] = ct_re * pr - ct_im * pi
    v_ref[0, p:, :] = -(ct_re * pi + ct_im * pr)

    pr, pi = _s5_power(lrc, lic, dt, t_idx)
    vj_re = ct_re * pr - ct_im * pi
    vj_im = ct_re * pi + ct_im * pr
    kf = (jnp.dot(bbt_re, vj_re, precision=HIGHEST, preferred_element_type=F32)
          - jnp.dot(bbt_im, vj_im, precision=HIGHEST, preferred_element_type=F32))
    lane = lax.broadcasted_iota(jnp.int32, (SSM_GROUP, S5_N), 1)
    for s in range(S5_CHUNK):
        blk = kf if s == 0 else pltpu.roll(kf, SSM_GROUP * s, 1)
        blk = jnp.where(lane >= SSM_GROUP * s, blk, 0.0)
        mt_ref[0, s * SSM_GROUP:(s + 1) * SSM_GROUP, :] = blk

    ar, ai = _s5_power(lrr, lir, dt, float(S5_CHUNK))
    a_ref[0] = jnp.concatenate([ar, ai], axis=1)


def _s5_prep(lam_re, lam_im, log_dt, b_re, b_im, c_re, c_im):
    g, p = lam_re.shape
    row = pl.BlockSpec((1, 1, p), lambda i: (i, 0, 0))
    col = pl.BlockSpec((1, p, 1), lambda i: (i, 0, 0))
    bt = pl.BlockSpec((1, SSM_GROUP, p), lambda i: (i, 0, 0))
    ct = pl.BlockSpec((1, p, SSM_GROUP), lambda i: (i, 0, 0))
    return pl.pallas_call(
        _s5_prep_kernel,
        out_shape=(jax.ShapeDtypeStruct((g, S5_N, S5_N), F32),
                   jax.ShapeDtypeStruct((g, S5_N, 2 * p), F32),
                   jax.ShapeDtypeStruct((g, 2 * p, S5_N), F32),
                   jax.ShapeDtypeStruct((g, 1, 2 * p), F32)),
        grid=(g,),
        in_specs=[row, row, col, col, pl.BlockSpec((1, 1, 1), lambda i: (i, 0, 0)), bt, bt, ct, ct],
        out_specs=(pl.BlockSpec((1, S5_N, S5_N), lambda i: (i, 0, 0)),
                   pl.BlockSpec((1, S5_N, 2 * p), lambda i: (i, 0, 0)),
                   pl.BlockSpec((1, 2 * p, S5_N), lambda i: (i, 0, 0)),
                   pl.BlockSpec((1, 1, 2 * p), lambda i: (i, 0, 0))),
        compiler_params=_params(("parallel",)),
        name="s5_chunk_operators",
    )(lam_re.reshape(g, 1, p), lam_im.reshape(g, 1, p), lam_re.reshape(g, p, 1), lam_im.reshape(g, p, 1),
      log_dt.reshape(g, 1, 1), jnp.swapaxes(b_re, 1, 2), jnp.swapaxes(b_im, 1, 2),
      jnp.swapaxes(c_re, 1, 2), jnp.swapaxes(c_im, 1, 2))


def _s5_block_operators(ops):
    mt, w, v, a = ops
    nb, gpb, c, h, p = S5_BLOCKS, S5_GPB, S5_CHUNK, SSM_GROUP, SSM_STATE
    eye = jnp.eye(gpb, dtype=F32)
    m_blk = jnp.einsum("jgabcd,gh->jagbchd", mt.reshape(nb, gpb, c, h, c, h), eye)
    w_blk = jnp.einsum("jgabrp,gh->jagbrhp", w.reshape(nb, gpb, c, h, 2, p), eye)
    v_blk = jnp.einsum("jgrpcd,gh->jrgpchd", v.reshape(nb, gpb, 2, p, c, h), eye)
    a2 = a.reshape(N_GROUPS, 2, p)
    return (m_blk.reshape(nb, S5_K, S5_K).astype(BF16), w_blk.reshape(nb, S5_K, 2 * S5_HALF).astype(BF16),
            v_blk.reshape(nb, 2 * S5_HALF, S5_K).astype(BF16),
            a2[:, 0].reshape(1, N_GROUPS * p), a2[:, 1].reshape(1, N_GROUPS * p))


def _s5_gather_chunks(u_ref, rows):
    return jnp.concatenate([u_ref[pl.ds(s, rows, stride=S5_CHUNK), :] for s in range(S5_CHUNK)], axis=1)


def _s5_local_kernel(u_ref, w_ref, x_ref, dre_ref, dim_ref):
    rows = x_ref.shape[0]
    x = _s5_gather_chunks(u_ref, rows).astype(BF16)
    x_ref[...] = x
    d = jnp.dot(x, w_ref[0], preferred_element_type=F32)
    dre_ref[...] = d[:, :S5_HALF]
    dim_ref[...] = d[:, S5_HALF:]


def _s5_local(u2, w_blk, rows):
    t = u2.shape[0]
    nc = t // S5_CHUNK
    return pl.pallas_call(
        _s5_local_kernel,
        out_shape=(jax.ShapeDtypeStruct((nc, S5_BLOCKS * S5_K), BF16),
                   jax.ShapeDtypeStruct((nc, S5_BLOCKS * S5_HALF), F32),
                   jax.ShapeDtypeStruct((nc, S5_BLOCKS * S5_HALF), F32)),
        grid=(S5_BLOCKS, nc // rows),
        in_specs=[pl.BlockSpec((rows * S5_CHUNK, LANES), lambda j, i: (i, j)),
                  pl.BlockSpec((1, S5_K, 2 * S5_HALF), lambda j, i: (j, 0, 0))],
        out_specs=(pl.BlockSpec((rows, S5_K), lambda j, i: (i, j)),
                   pl.BlockSpec((rows, S5_HALF), lambda j, i: (i, j)),
                   pl.BlockSpec((rows, S5_HALF), lambda j, i: (i, j))),
        compiler_params=_params(("parallel", "parallel")),
        name="s5_chunk_state_increment",
    )(u2, w_blk)


def _s5_scan_kernel(dre_ref, dim_ref, h0re_ref, h0im_ref, ar_ref, ai_ref, hre_ref, him_ref, fre_ref, fim_ref,
                    cre_ref, cim_ref):
    @pl.when(pl.program_id(2) == 0)
    def _():
        cre_ref[...] = h0re_ref[0]
        cim_ref[...] = h0im_ref[0]

    ar = ar_ref[...]
    ai = ai_ref[...]

    def body(r, h):
        hr, hi = h
        hre_ref[0, pl.ds(r, 1), :] = hr
        him_ref[0, pl.ds(r, 1), :] = hi
        dr = dre_ref[0, pl.ds(r, 1), :]
        di = dim_ref[0, pl.ds(r, 1), :]
        return ar * hr - ai * hi + dr, ar * hi + ai * hr + di

    hr, hi = lax.fori_loop(0, dre_ref.shape[1], body, (cre_ref[...], cim_ref[...]))
    cre_ref[...] = hr
    cim_ref[...] = hi
    fre_ref[0] = hr
    fim_ref[0] = hi


def _s5_scan(d_re, d_im, h0_re, h0_im, a_re, a_im, tc, tw):
    b, ncb, w = d_re.shape
    dspec = pl.BlockSpec((1, tc, tw), lambda i, k, c: (i, c, k))
    hspec = pl.BlockSpec((1, 1, tw), lambda i, k, c: (i, 0, k))
    aspec = pl.BlockSpec((1, tw), lambda i, k, c: (0, k))
    big = jax.ShapeDtypeStruct((b, ncb, w), F32)
    small = jax.ShapeDtypeStruct((b, 1, w), F32)
    return pl.pallas_call(
        _s5_scan_kernel,
        out_shape=(big, big, small, small),
        grid=(b, w // tw, ncb // tc),
        in_specs=[dspec, dspec, hspec, hspec, aspec, aspec],
        out_specs=(dspec, dspec, hspec, hspec),
        scratch_shapes=[pltpu.VMEM((1, tw), F32), pltpu.VMEM((1, tw), F32)],
        compiler_params=_params(("parallel", "parallel", "arbitrary")),
        name="s5_chunk_state_scan",
    )(d_re, d_im, h0_re, h0_im, a_re, a_im)


def _gelu_tanh(x):
    return x * (0.5 * (1.0 + jnp.tanh(math.sqrt(2.0 / math.pi) * (x + 0.044715 * (x * x * x)))))


def _s5_out_kernel(x_ref, hre_ref, him_ref, m_ref, v_ref, u_ref, d_ref, z_ref, z_scr):
    rows = x_ref.shape[0]
    hre = hre_ref[...].astype(BF16)
    him = him_ref[...].astype(BF16)
    skip = d_ref[...]
    pair = 2 * LANES
    for q in range(S5_K // pair):
        cols = slice(q * pair, (q + 1) * pair)
        k = (q + 1) * pair
        y = (jnp.dot(x_ref[:, :k], m_ref[0, :k, cols], preferred_element_type=F32)
             + jnp.dot(hre, v_ref[0, :S5_HALF, cols], preferred_element_type=F32)
             + jnp.dot(him, v_ref[0, S5_HALF:, cols], preferred_element_type=F32))
        for e in range(2):
            tok = pl.ds(2 * q + e, rows, stride=S5_CHUNK)
            z_scr[tok, :] = _gelu_tanh(y[:, e * LANES:(e + 1) * LANES] + skip * u_ref[tok, :])
    z_ref[...] = z_scr[...].astype(BF16)


def _s5_out(x, h_re, h_im, m_blk, v_blk, u2, ssm_d, rows):
    nc = x.shape[0]
    t = nc * S5_CHUNK
    return pl.pallas_call(
        _s5_out_kernel,
        out_shape=jax.ShapeDtypeStruct((t, SSM_WIDTH), BF16),
        grid=(S5_BLOCKS, nc // rows),
        in_specs=[pl.BlockSpec((rows, S5_K), lambda j, i: (i, j)),
                  pl.BlockSpec((rows, S5_HALF), lambda j, i: (i, j)),
                  pl.BlockSpec((rows, S5_HALF), lambda j, i: (i, j)),
                  pl.BlockSpec((1, S5_K, S5_K), lambda j, i: (j, 0, 0)),
                  pl.BlockSpec((1, 2 * S5_HALF, S5_K), lambda j, i: (j, 0, 0)),
                  pl.BlockSpec((rows * S5_CHUNK, LANES), lambda j, i: (i, j)),
                  pl.BlockSpec((1, LANES), lambda j, i: (0, j))],
        out_specs=pl.BlockSpec((rows * S5_CHUNK, LANES), lambda j, i: (i, j)),
        scratch_shapes=[pltpu.VMEM((rows * S5_CHUNK, LANES), F32)],
        compiler_params=_params(("parallel", "parallel")),
        name="s5_chunk_output",
    )(x, h_re, h_im, m_blk, v_blk, u2, ssm_d.reshape(1, SSM_WIDTH))


def _state_to_blocks(h):
    return h.astype(F32).reshape(h.shape[0], 1, N_GROUPS * SSM_STATE)


def _s5(u, h0_re, h0_im, blk_ops, ssm_d):
    m_blk, w_blk, v_blk, a_re, a_im = blk_ops
    b, l, _ = u.shape
    ncb = l // S5_CHUNK
    w = N_GROUPS * SSM_STATE
    u2 = u.reshape(b * l, SSM_WIDTH)
    rows = min(b * ncb, 256)
    x, d_re, d_im = _s5_local(u2, w_blk, rows)
    if h0_re is None:
        h0_re = h0_im = jnp.zeros((b, N_GROUPS, SSM_STATE), F32)
    hs_re, hs_im, hf_re, hf_im = _s5_scan(d_re.reshape(b, ncb, w), d_im.reshape(b, ncb, w),
                                          _state_to_blocks(h0_re), _state_to_blocks(h0_im), a_re, a_im,
                                          tc=min(ncb, 64), tw=1024)
    z = _s5_out(x, hs_re.reshape(b * ncb, w), hs_im.reshape(b * ncb, w), m_blk, v_blk, u2, ssm_d, rows)
    return (z.reshape(b, l, SSM_WIDTH), hf_re.reshape(b, N_GROUPS, SSM_STATE), hf_im.reshape(b, N_GROUPS, SSM_STATE))


def _mix_kernel(x_ref, z_ref, o_ref, gate_ref, wg_ref, bg_ref, wo_ref, out_ref):
    nb, tl, d = x_ref.shape
    rows = nb * tl
    gl = jnp.dot(z_ref[...].reshape(rows, SSM_WIDTH), wg_ref[...], preferred_element_type=F32) + bg_ref[...]
    yg = (gl[:, :SSM_WIDTH] * _sigmoid(gl[:, SSM_WIDTH:])).astype(BF16)
    mix = (jnp.dot(o_ref[...].reshape(rows, ATTN_WIDTH), wo_ref[:ATTN_WIDTH, :], preferred_element_type=F32)
           + jnp.dot(yg, wo_ref[ATTN_WIDTH:, :], preferred_element_type=F32))
    out_ref[...] = x_ref[...] + gate_ref[...] * mix.reshape(nb, tl, d)


def _mix(x, z_ssm, o_attn, gate, w_glu_b, b_glu, w_out_b, nb, tl):
    b, l, d = x.shape
    xspec = lambda w: pl.BlockSpec((nb, tl, w), lambda i, j: (i, j, 0))
    const = lambda shape: pl.BlockSpec(shape, lambda i, j: (0, 0), pipeline_mode=pl.Buffered(1))
    return pl.pallas_call(
        _mix_kernel,
        out_shape=jax.ShapeDtypeStruct((b, l, d), F32),
        grid=(b // nb, l // tl),
        in_specs=[xspec(d), xspec(SSM_WIDTH), xspec(ATTN_WIDTH),
                  pl.BlockSpec((nb, 1, d), lambda i, j: (i, 0, 0)),
                  const((SSM_WIDTH, 2 * SSM_WIDTH)), const((1, 2 * SSM_WIDTH)), const((d, d))],
        out_specs=xspec(d),
        compiler_params=_params(("parallel", "parallel")),
        name="glu_outproj_residual",
    )(x, z_ssm, o_attn, gate, w_glu_b, b_glu.reshape(1, -1), w_out_b)


FFN_ROWS = 1024
FFN_TF = 512

def _ffn_kernel(x_ref, sc_ref, sh_ref, gate_ref, g_ref, gf_ref, w1_ref, w2_ref, y_ref, h_ref, acc_ref):
    nb, tl, d = x_ref.shape
    j = pl.program_id(2)

    @pl.when(j == 0)
    def _():
        h = _modulated_norm(x_ref[...], g_ref[...], sc_ref[...], sh_ref[...])
        h_ref[...] = h.reshape(nb * tl, d).astype(BF16)
        acc_ref[...] = jnp.zeros_like(acc_ref)

    hid = jnp.dot(h_ref[...], w1_ref[...], preferred_element_type=F32)
    hid = jnp.square(jnp.maximum(hid, 0.0)).astype(BF16)
    acc_ref[...] += jnp.dot(hid, w2_ref[...], preferred_element_type=F32)

    @pl.when(j == pl.num_programs(2) - 1)
    def _():
        x = x_ref[...] + gate_ref[...] * acc_ref[...].reshape(nb, tl, d)
        ms = jnp.mean(x * x, axis=-1, keepdims=True)
        y_ref[...] = x * lax.rsqrt(ms + EPS) * gf_ref[...]


def _ffn(x, scale, shift, gate, g_ffn, g_final, w1_b, w2_b, nb, tl, tf):
    b, l, d = x.shape
    xspec = pl.BlockSpec((nb, tl, d), lambda i, j, k: (i, j, 0))
    xin = pl.BlockSpec((nb, tl, d), lambda i, j, k: (i, j, 0), pipeline_mode=pl.Buffered(1))
    mspec = pl.BlockSpec((nb, 1, d), lambda i, j, k: (i, 0, 0))
    gspec = pl.BlockSpec((1, d), lambda i, j, k: (0, 0))
    return pl.pallas_call(
        _ffn_kernel,
        out_shape=jax.ShapeDtypeStruct((b, l, d), F32),
        grid=(b // nb, l // tl, D_FF // tf),
        in_specs=[xin, mspec, mspec, mspec, gspec, gspec,
                  pl.BlockSpec((d, tf), lambda i, j, k: (0, k)),
                  pl.BlockSpec((tf, d), lambda i, j, k: (k, 0))],
        out_specs=xspec,
        scratch_shapes=[pltpu.VMEM((nb * tl, d), BF16), pltpu.VMEM((nb * tl, d), F32)],
        compiler_params=_params(("parallel", "parallel", "arbitrary")),
        name="relu2_mlp_final_norm",
    )(x, scale, shift, gate, g_ffn.reshape(1, d), g_final.reshape(1, d), w1_b, w2_b)


def _layer(x, mods, tables, tables_t, past_kt, past_v2, h0_re, h0_im, wts, nb, tl):
    shift1, scale1, gate1, shift2, scale2, gate2 = mods
    b, l, _ = x.shape
    first_chunk = past_kt is None
    w_k = wts["w_k_t"] if first_chunk else wts["w_k"]
    q, k, v, u, kb, vb = _inproj(x, scale1, shift1, wts["g_mix"], wts["w_qvu"], w_k, tables, tables_t, nb, tl)
    if first_chunk:
        o = _attn_prompt(q, kb, vb, wts["lam_vecs"], wts["g_subln"])
        k6 = k.reshape(b, N_HEADS, 2, HEAD_DIM, l).transpose(0, 4, 1, 2, 3)[None]
    else:
        o = _attn_sample(q, past_kt, past_v2, kb, vb, wts["lam_vecs"], wts["g_subln"])
        k6 = k.reshape(1, b, l, N_HEADS, 2, HEAD_DIM)
    z_ssm, h_re, h_im = _s5(u, h0_re, h0_im, wts["s5_ops"], wts["ssm_d"])
    x1 = _mix(x, z_ssm, o, gate1, wts["w_glu"], wts["b_glu"], wts["w_out"], nb, tl)
    tl_ffn = min(l, FFN_ROWS) if nb == 1 else tl
    y = _ffn(x1, scale2, shift2, gate2, wts["g_ffn"], wts["g_final"], wts["w_ff1"], wts["w_ff2"],
             nb, tl_ffn, tf=FFN_TF)
    v5 = v.reshape(1, b, l, N_HEADS, V_DIM)
    return y, k6, v5, h_re[None], h_im[None]


def kernel(x_prompt, x_sample, c_prompt, c_sample, cache_k, cache_v, state_ssm_re, state_ssm_im, w_ada, b_ada, g_mix, w_in, lam_q1, lam_k1, lam_q2, lam_k2, g_subln, ssm_lam_re, ssm_lam_im, ssm_log_dt, ssm_b_re, ssm_b_im, ssm_c_re, ssm_c_im, ssm_d, w_glu, b_glu, w_out, g_ffn, w_ff1, w_ff2, g_final):
    bp, lp, d = x_prompt.shape
    bs, ls, _ = x_sample.shape
    past = cache_k.shape[2]

    m = _ada(jnp.concatenate([c_prompt, c_sample], axis=0), w_ada[0], b_ada[0])
    mods = [m[:, i * d:(i + 1) * d].reshape(bp + bs, 1, d) for i in range(6)]
    mods_p = [t[:bp] for t in mods]
    mods_s = [t[bp:] for t in mods]

    w_in_b = w_in[0].astype(BF16)
    w_k = w_in_b[:, ATTN_WIDTH:2 * ATTN_WIDTH]
    wts = {
        "g_mix": g_mix[0],
        "w_qvu": jnp.concatenate([w_in_b[:, :ATTN_WIDTH], w_in_b[:, 2 * ATTN_WIDTH:]], axis=1),
        "w_k": w_k, "w_k_t": w_k.T,
        "lam_vecs": tuple(t[0].reshape(1, HEAD_DIM) for t in (lam_q1, lam_k1, lam_q2, lam_k2)),
        "g_subln": g_subln[0],
        "s5_ops": _s5_block_operators(_s5_prep(ssm_lam_re[0], ssm_lam_im[0], ssm_log_dt[0], ssm_b_re[0],
                                                ssm_b_im[0], ssm_c_re[0], ssm_c_im[0])),
        "ssm_d": ssm_d[0], "w_glu": w_glu[0].astype(BF16), "b_glu": b_glu[0],
        "w_out": w_out[0].astype(BF16), "g_ffn": g_ffn[0], "g_final": g_final,
        "w_ff1": w_ff1[0].astype(BF16), "w_ff2": w_ff2[0].astype(BF16),
    }

    tl_p = min(lp, 512)
    yp, kp, vp, srp, sip = _layer(x_prompt, mods_p, _rope_tables(lp, 0), _rope_tables_t(lp, 0),
                                  None, None, None, None, wts, nb=1, tl=tl_p)
    nb_s = max(1, min(bs, 512 // ls))
    ckt = cache_k[0].reshape(bs, past, ATTN_WIDTH).transpose(0, 2, 1)
    cv2 = cache_v[0].reshape(bs, past * N_HEADS, V_DIM)
    ys, ks, vs, srs, sis = _layer(x_sample, mods_s, _rope_tables(ls, past), None, ckt, cv2,
                                  state_ssm_re[0], state_ssm_im[0], wts, nb=nb_s, tl=ls)
    return (yp, ys, kp, vp, srp, sip, ks, vs, srs, sis)
```

```python
import functools
import math

import jax
import jax.numpy as jnp
from jax import lax
from jax.experimental import pallas as pl
from jax.experimental.pallas import tpu as pltpu

F32 = jnp.float32
BF16 = jnp.bfloat16

D_MODEL = 2048
ATTN_WIDTH = D_MODEL // 2
SSM_WIDTH = D_MODEL - ATTN_WIDTH
N_HEADS = 8
HEAD_DIM = ATTN_WIDTH // (2 * N_HEADS)
V_DIM = 2 * HEAD_DIM
ROT_DIM = HEAD_DIM // 4
ROPE_THETA = 500000.0
CHUNK = 64
SSM_GROUP = 16
N_GROUPS = SSM_WIDTH // SSM_GROUP
SSM_STATE = 64
IN_WIDTH = 3 * ATTN_WIDTH + SSM_WIDTH
D_FF = 4 * D_MODEL
EPS = 1e-6
NEG_INF = -1e30
LAM_INIT = 0.8 - 0.6 * math.exp(-0.3 * 0)

LANES = 128
S5_CHUNK = 16
S5_BLOCKS = SSM_WIDTH // LANES
S5_GPB = LANES // SSM_GROUP
S5_K = S5_CHUNK * LANES
S5_HALF = S5_GPB * SSM_STATE
VMEM_LIMIT = 56 * 1024 * 1024

HIGHEST = lax.Precision.HIGHEST


def _params(sem, vmem=VMEM_LIMIT):
    return pltpu.CompilerParams(dimension_semantics=sem, vmem_limit_bytes=vmem)


def _sigmoid(x):
    return 1.0 / (1.0 + jnp.exp(-x))


def _ada_kernel(c_ref, w_ref, b_ref, o_ref):
    c = c_ref[...]
    s = (c * _sigmoid(c)).astype(BF16)
    o_ref[...] = jnp.dot(s, w_ref[...].astype(BF16), preferred_element_type=F32) + b_ref[...]


def _ada(c_all, w_ada, b_ada):
    nb, d = c_all.shape
    n = w_ada.shape[1]
    tn = 1024
    return pl.pallas_call(
        _ada_kernel,
        out_shape=jax.ShapeDtypeStruct((nb, n), F32),
        grid=(n // tn,),
        in_specs=[pl.BlockSpec((nb, d), lambda j: (0, 0)),
                  pl.BlockSpec((d, tn), lambda j: (0, j)),
                  pl.BlockSpec((1, tn), lambda j: (0, j))],
        out_specs=pl.BlockSpec((nb, tn), lambda j: (0, j)),
        compiler_params=_params(("parallel",)),
        name="ada_modulation",
    )(c_all, w_ada, b_ada.reshape(1, n))


def _rope_angles(pos, freq_index):
    inv = jnp.exp(freq_index * (-2.0 * math.log(ROPE_THETA) / ROT_DIM))
    return pos * inv


def _rope_kernel(c_ref, sa_ref, sb_ref, *, offset):
    tl = c_ref.shape[0]
    row = lax.broadcasted_iota(jnp.int32, (tl, LANES), 0)
    lane = lax.broadcasted_iota(jnp.int32, (tl, LANES), 1)
    pos = (offset + pl.program_id(0) * tl + row).astype(F32)
    d = lane % HEAD_DIM
    ang = _rope_angles(pos, (d % (ROT_DIM // 2)).astype(F32))
    c = jnp.cos(ang)
    s = jnp.sin(ang)
    in_rot = d < ROT_DIM
    first = d < ROT_DIM // 2
    c_ref[...] = jnp.where(in_rot, c, 1.0)
    sa_ref[...] = jnp.where(in_rot, jnp.where(first, 0.0, s), 0.0)
    sb_ref[...] = jnp.where(first, -s, 0.0)


def _rope_tables(length, offset):
    tl = min(length, 1024)
    shp = jax.ShapeDtypeStruct((length, LANES), F32)
    spec = pl.BlockSpec((tl, LANES), lambda i: (i, 0))
    return pl.pallas_call(
        functools.partial(_rope_kernel, offset=offset),
        out_shape=(shp, shp, shp),
        grid=(length // tl,),
        out_specs=(spec, spec, spec),
        compiler_params=_params(("parallel",)),
        name="rope_tables",
    )()


def _rope_t_kernel(c_ref, s_ref, *, offset):
    n = c_ref.shape[1]
    fi = lax.broadcasted_iota(jnp.int32, (ROT_DIM // 2, n), 0).astype(F32)
    pos = (offset + pl.program_id(0) * n + lax.broadcasted_iota(jnp.int32, (ROT_DIM // 2, n), 1)).astype(F32)
    ang = _rope_angles(pos, fi)
    c_ref[...] = jnp.cos(ang)
    s_ref[...] = jnp.sin(ang)


def _rope_tables_t(length, offset):
    tn = min(length, 2048)
    shp = jax.ShapeDtypeStruct((ROT_DIM // 2, length), F32)
    spec = pl.BlockSpec((ROT_DIM // 2, tn), lambda i: (0, i))
    return pl.pallas_call(
        functools.partial(_rope_t_kernel, offset=offset),
        out_shape=(shp, shp),
        grid=(length // tn,),
        out_specs=(spec, spec),
        compiler_params=_params(("parallel",)),
        name="rope_tables_transposed",
    )()


Q_SCALE = HEAD_DIM ** -0.5 * math.log2(math.e)


def _modulated_norm(x, g, scale, shift):
    ms = jnp.mean(x * x, axis=-1, keepdims=True)
    y = x * lax.rsqrt(ms + EPS) * g
    return y * (1.0 + scale) + shift


def _inproj_kernel(*refs, k_t):
    if k_t:
        (x_ref, sc_ref, sh_ref, g_ref, w_ref, wk_ref, c_ref, sa_ref, sb_ref, ct_ref, st_ref,
         q_ref, k_ref, v_ref, u_ref, kb_ref, vb_ref) = refs
    else:
        (x_ref, sc_ref, sh_ref, g_ref, w_ref, wk_ref, c_ref, sa_ref, sb_ref,
         q_ref, k_ref, v_ref, u_ref, kb_ref, vb_ref) = refs
    nb, tl, d = x_ref.shape
    rows = nb * tl
    h = _modulated_norm(x_ref[...], g_ref[...], sc_ref[...], sh_ref[...])
    hb = h.reshape(rows, d).astype(BF16)
    cos = c_ref[...]
    sin_a = sa_ref[...]
    sin_b = sb_ref[...]

    def rope(t):
        ra = pltpu.roll(t, ROT_DIM // 2, 1).reshape(nb, tl, LANES)
        rb = pltpu.roll(t, LANES - ROT_DIM // 2, 1).reshape(nb, tl, LANES)
        return t.reshape(nb, tl, LANES) * cos + ra * sin_a + rb * sin_b

    q = jnp.dot(hb, w_ref[:, 0:ATTN_WIDTH], preferred_element_type=F32)
    for hd in range(N_HEADS):
        sl = slice(hd * LANES, (hd + 1) * LANES)
        q_ref[:, :, sl] = (rope(q[:, sl]) * Q_SCALE).astype(BF16)
    if k_t:
        kt = _nt_dot(wk_ref[...], hb)
        cos_t = ct_ref[...]
        sin_t = st_ref[...]
        half = ROT_DIM // 2
        for comp in range(2 * N_HEADS):
            b0 = comp * HEAD_DIM
            t1 = kt[b0:b0 + half]
            t2 = kt[b0 + half:b0 + ROT_DIM]
            blk = jnp.concatenate([t1 * cos_t - t2 * sin_t, t2 * cos_t + t1 * sin_t,
                                   kt[b0 + ROT_DIM:b0 + HEAD_DIM]], axis=0)
            k_ref[0, b0:b0 + HEAD_DIM, :] = blk
            kb_ref[0, b0:b0 + HEAD_DIM, :] = blk.astype(BF16)
    else:
        k = jnp.dot(hb, wk_ref[...], preferred_element_type=F32)
        for hd in range(N_HEADS):
            sl = slice(hd * LANES, (hd + 1) * LANES)
            kr = rope(k[:, sl])
            k_ref[:, :, sl] = kr
            kb_ref[:, :, sl] = kr.astype(BF16)
    v = jnp.dot(hb, w_ref[:, ATTN_WIDTH:2 * ATTN_WIDTH], preferred_element_type=F32)
    v = v.reshape(nb, tl, ATTN_WIDTH)
    v_ref[...] = v
    vb_ref[...] = v.astype(BF16)
    u = jnp.dot(hb, w_ref[:, 2 * ATTN_WIDTH:], preferred_element_type=F32)
    u_ref[...] = u.reshape(nb, tl, SSM_WIDTH)


def _inproj(x, scale, shift, g_mix, w_qvu, w_k, tables, tables_t, nb, tl):
    b, l, d = x.shape
    k_t = tables_t is not None
    assert not k_t or nb == 1
    grid = (b // nb, l // tl)
    xspec = lambda w: pl.BlockSpec((nb, tl, w), lambda i, j: (i, j, 0))
    mspec = pl.BlockSpec((nb, 1, d), lambda i, j: (i, 0, 0))
    tspec = pl.BlockSpec((tl, LANES), lambda i, j: (j, 0))
    const = lambda a: pl.BlockSpec(a.shape, lambda i, j: (0, 0), pipeline_mode=pl.Buffered(1))
    out = lambda dt: jax.ShapeDtypeStruct((b, l, ATTN_WIDTH), dt)
    in_specs = [xspec(d), mspec, mspec, pl.BlockSpec((1, d), lambda i, j: (0, 0)),
                const(w_qvu), const(w_k), tspec, tspec, tspec]
    args = [x, scale, shift, g_mix.reshape(1, d), w_qvu, w_k, *tables]
    kspec, kshape = xspec(ATTN_WIDTH), out
    if k_t:
        in_specs += [pl.BlockSpec((ROT_DIM // 2, tl), lambda i, j: (0, j))] * 2
        args += list(tables_t)
        kspec = pl.BlockSpec((1, ATTN_WIDTH, tl), lambda i, j: (i, 0, j))
        kshape = lambda dt: jax.ShapeDtypeStruct((b, ATTN_WIDTH, l), dt)
    return pl.pallas_call(
        functools.partial(_inproj_kernel, k_t=k_t),
        out_shape=(out(BF16), kshape(F32), out(F32), out(F32), kshape(BF16), out(BF16)),
        grid=grid,
        in_specs=in_specs,
        out_specs=(xspec(ATTN_WIDTH), kspec, xspec(ATTN_WIDTH), xspec(ATTN_WIDTH), kspec, xspec(ATTN_WIDTH)),
        compiler_params=_params(("parallel", "parallel")),
        name="norm_inproj_rope",
    )(*args)


ATTN_TQ = 512
ATTN_TK = 512
ATTN_NSUB = 2


def _stack_components(q):
    lane = lax.broadcasted_iota(jnp.int32, q.shape, 1)
    zero = jnp.zeros_like(q)
    return jnp.concatenate([jnp.where(lane < HEAD_DIM, q, zero),
                            jnp.where(lane >= HEAD_DIM, q, zero)], axis=0)


def _diff_epilogue(num, den, tq, lq1, lk1, lq2, lk2, g_subln):
    lam = (jnp.exp(jnp.sum(lq1 * lk1, axis=-1, keepdims=True))
           - jnp.exp(jnp.sum(lq2 * lk2, axis=-1, keepdims=True)) + LAM_INIT)
    on = num / den
    o = on[:tq] - lam * on[tq:]
    ms = jnp.mean(o * o, axis=-1, keepdims=True)
    return o * lax.rsqrt(ms + EPS) * g_subln * (1.0 - LAM_INIT)


def _nt_dot(a, b):
    return lax.dot_general(a, b, (((1,), (1,)), ((), ())), preferred_element_type=F32)


def _attn_prompt_kernel(lq1_ref, lk1_ref, lq2_ref, lk2_ref, gs_ref, q_ref, kt_ref, v_ref, o_ref,
                        qs_scr, s_scr, p_scr, m_scr, acc_scr, *, tq, tk, nsub):
    i = pl.program_id(2)
    rows = 2 * tq
    qs_scr[...] = _stack_components(q_ref[0])
    span = nsub * tk
    nfull = (i * tq) // span
    ones = jnp.ones((tk, LANES), BF16)
    m_scr[...] = jnp.full((rows, LANES), NEG_INF, F32)
    acc_scr[...] = jnp.zeros((rows, 2 * LANES), F32)

    def scores(start, u, masked):
        s = jnp.dot(qs_scr[...], kt_ref[0, :, pl.ds(start, tk)], preferred_element_type=F32)
        if masked:
            r = lax.broadcasted_iota(jnp.int32, s.shape, 0)
            c = lax.broadcasted_iota(jnp.int32, s.shape, 1)
            qc = (i * tq + jnp.where(r >= tq, r - tq, r)) // CHUNK
            s = jnp.where((start + c) // CHUNK <= qc, s, NEG_INF)
        s_scr[u] = s
        return jnp.broadcast_to(jnp.max(s, axis=-1, keepdims=True), (rows, LANES))

    def accumulate(start, u, block_max):
        m_old = m_scr[...]
        m_new = jnp.maximum(m_old, block_max)
        alpha = jnp.exp2(m_old - m_new)
        m_scr[...] = m_new
        p = jnp.exp2(s_scr[u] - jnp.concatenate([m_new] * (tk // LANES), axis=1))
        p_scr[u] = p.astype(BF16)
        v1 = jnp.concatenate([v_ref[0, pl.ds(start, tk), :], ones], axis=1)
        pv = jnp.dot(p_scr[u], v1, preferred_element_type=F32)
        acc_scr[...] = jnp.concatenate([alpha, alpha], axis=1) * acc_scr[...] + pv

    def body(t, _):
        base = pl.multiple_of(t * span, span)
        maxes = [scores(base + u * tk, u, False) for u in range(nsub)]
        for u in range(nsub):
            accumulate(base + u * tk, u, maxes[u])
        return 0
    lax.fori_loop(0, nfull, body, 0)

    left = i - nfull * nsub
    for n_left in range(nsub):
        @pl.when(left == n_left)
        def _(n_left=n_left):
            base = pl.multiple_of(nfull * span, span)
            maxes = [scores(base + u * tk, u, u == n_left) for u in range(n_left + 1)]
            for u in range(n_left + 1):
                accumulate(base + u * tk, u, maxes[u])

    acc = acc_scr[...]
    o = _diff_epilogue(acc[:, :LANES], acc[:, LANES:], tq, lq1_ref[...], lk1_ref[...], lq2_ref[...],
                       lk2_ref[...], gs_ref[...])
    o_ref[0] = o.astype(BF16)


def _attn_prompt(q, ktb, vb, lam_vecs, g_subln):
    b, l, _ = q.shape
    tq = min(l, ATTN_TQ)
    tk = min(l, ATTN_TK)
    nsub = ATTN_NSUB
    assert l % tq == 0 and tq == tk and tq % CHUNK == 0
    vec = pl.BlockSpec((1, HEAD_DIM), lambda bi, h, i: (0, 0))
    return pl.pallas_call(
        functools.partial(_attn_prompt_kernel, tq=tq, tk=tk, nsub=nsub),
        out_shape=jax.ShapeDtypeStruct((b, l, ATTN_WIDTH), BF16),
        grid=(b, N_HEADS, l // tq),
        in_specs=[vec, vec, vec, vec,
                  pl.BlockSpec((1, V_DIM), lambda bi, h, i: (0, 0)),
                  pl.BlockSpec((1, tq, LANES), lambda bi, h, i: (bi, i, h)),
                  pl.BlockSpec((1, LANES, l), lambda bi, h, i: (bi, h, 0)),
                  pl.BlockSpec((1, l, LANES), lambda bi, h, i: (bi, 0, h))],
        out_specs=pl.BlockSpec((1, tq, LANES), lambda bi, h, i: (bi, i, h)),
        scratch_shapes=[pltpu.VMEM((2 * tq, LANES), BF16), pltpu.VMEM((nsub, 2 * tq, tk), F32),
                        pltpu.VMEM((nsub, 2 * tq, tk), BF16),
                        pltpu.VMEM((2 * tq, LANES), F32), pltpu.VMEM((2 * tq, 2 * LANES), F32)],
        compiler_params=_params(("parallel", "parallel", "arbitrary")),
        name="prompt_diff_attention",
    )(*lam_vecs, g_subln.reshape(1, V_DIM), q, ktb, vb)


def _attn_sample_kernel(lq1_ref, lk1_ref, lq2_ref, lk2_ref, gs_ref, q_ref, ckt_ref, cv_ref,
                        kn_ref, vn_ref, o_ref):
    tq = q_ref.shape[1]
    past = ckt_ref.shape[2]
    for hd in range(N_HEADS):
        cols = slice(hd * LANES, (hd + 1) * LANES)
        qs = _stack_components(q_ref[0, :, cols])
        s_p = jnp.dot(qs, ckt_ref[0, cols, :].astype(BF16), preferred_element_type=F32)
        s_n = _nt_dot(qs, kn_ref[0, :, cols])
        m = jnp.maximum(jnp.max(s_p, axis=-1, keepdims=True), jnp.max(s_n, axis=-1, keepdims=True))
        p_p = jnp.exp2(s_p - m)
        p_n = jnp.exp2(s_n - m)
        l = jnp.sum(p_p, axis=-1, keepdims=True) + jnp.sum(p_n, axis=-1, keepdims=True)
        vp = cv_ref[0, pl.ds(hd, past, stride=N_HEADS), :].astype(BF16)
        acc = (jnp.dot(p_p.astype(BF16), vp, preferred_element_type=F32)
               + jnp.dot(p_n.astype(BF16), vn_ref[0, :, cols], preferred_element_type=F32))
        o = _diff_epilogue(acc, l, tq, lq1_ref[...], lk1_ref[...], lq2_ref[...], lk2_ref[...], gs_ref[...])
        o_ref[0, :, cols] = o.astype(BF16)


def _attn_sample(q, cache_kt, cache_v2, kb, vb, lam_vecs, g_subln):
    b, l, _ = q.shape
    past = cache_kt.shape[2]
    vec = pl.BlockSpec((1, HEAD_DIM), lambda bi: (0, 0))
    new = pl.BlockSpec((1, l, ATTN_WIDTH), lambda bi: (bi, 0, 0))
    return pl.pallas_call(
        _attn_sample_kernel,
        out_shape=jax.ShapeDtypeStruct((b, l, ATTN_WIDTH), BF16),
        grid=(b,),
        in_specs=[vec, vec, vec, vec, pl.BlockSpec((1, V_DIM), lambda bi: (0, 0)),
                  new,
                  pl.BlockSpec((1, ATTN_WIDTH, past), lambda bi: (bi, 0, 0)),
                  pl.BlockSpec((1, past * N_HEADS, LANES), lambda bi: (bi, 0, 0)),
                  new, new],
        out_specs=new,
        compiler_params=_params(("parallel",)),
        name="sample_diff_attention",
    )(*lam_vecs, g_subln.reshape(1, V_DIM), q, cache_kt, cache_v2, kb, vb)


def _s5_discretize(lr, li, dt):
    mag = jnp.exp(lr * dt)
    ar = mag * jnp.cos(li * dt)
    ai = mag * jnp.sin(li * dt)
    den = lr * lr + li * li
    f_re = ((ar - 1.0) * lr + ai * li) / den
    f_im = (ai * lr - (ar - 1.0) * li) / den
    return f_re, f_im


def _s5_power(lr, li, dt, e):
    mag = jnp.exp(lr * dt * e)
    ang = li * dt * e
    return mag * jnp.cos(ang), mag * jnp.sin(ang)


def _place(x, offset, width):
    c = x.shape[1]
    row = lax.broadcasted_iota(jnp.int32, (c, width), 0)
    lane = lax.broadcasted_iota(jnp.int32, (c, width), 1)
    return jnp.dot(x, (lane == row + offset).astype(F32), precision=HIGHEST, preferred_element_type=F32)


def _s5_prep_kernel(lrr_ref, lir_ref, lrc_ref, lic_ref, ldtr_ref, ldtc_ref, brt_ref, bit_ref, br_ref, bi_ref,
                    cr_ref, ci_ref, crt_ref, cit_ref, m_ref, w_ref, v_ref, are_ref, aim_ref):
    gpb, p, hh, c = S5_GPB, SSM_STATE, SSM_GROUP, S5_CHUNK
    sp = gpb * p

    lrr = jnp.concatenate([lrr_ref[g] for g in range(gpb)], axis=1)
    lir = jnp.concatenate([lir_ref[g] for g in range(gpb)], axis=1)
    dtr = jnp.exp(jnp.concatenate([jnp.broadcast_to(ldtr_ref[g], (1, p)) for g in range(gpb)], axis=1))
    lrc = lrc_ref[...].reshape(sp, 1)
    lic = lic_ref[...].reshape(sp, 1)
    dtc = jnp.exp(jnp.broadcast_to(ldtc_ref[...], (gpb, p, 1)).reshape(sp, 1))
    f_re_r, f_im_r = _s5_discretize(lrr, lir, dtr)
    f_re_c, f_im_c = _s5_discretize(lrc, lic, dtc)

    bt_re = jnp.concatenate([_place(brt_ref[g], g * p, sp) for g in range(gpb)], axis=0)
    bt_im = jnp.concatenate([_place(bit_ref[g], g * p, sp) for g in range(gpb)], axis=0)
    b_re = jnp.concatenate([_place(br_ref[g], g * hh, LANES) for g in range(gpb)], axis=0)
    b_im = jnp.concatenate([_place(bi_ref[g], g * hh, LANES) for g in range(gpb)], axis=0)
    c_re = jnp.concatenate([_place(cr_ref[g], g * p, sp) for g in range(gpb)], axis=0)
    c_im = jnp.concatenate([_place(ci_ref[g], g * p, sp) for g in range(gpb)], axis=0)
    ct_re = jnp.concatenate([_place(crt_ref[g], g * hh, LANES) for g in range(gpb)], axis=0)
    ct_im = jnp.concatenate([_place(cit_ref[g], g * hh, LANES) for g in range(gpb)], axis=0)
    bbt_re = f_re_r * bt_re - f_im_r * bt_im
    bbt_im = f_re_r * bt_im + f_im_r * bt_re
    bb_re = f_re_c * b_re - f_im_c * b_im
    bb_im = f_re_c * b_im + f_im_c * b_re

    e_row = lax.broadcasted_iota(jnp.int32, (c + 1, 1), 0).astype(F32)
    pr_r, pi_r = _s5_power(lrr, lir, dtr, e_row)
    e_lane = lax.broadcasted_iota(jnp.int32, (1, LANES), 1)
    e_lane = jnp.where(e_lane <= c, e_lane, 0).astype(F32)
    pr_c, pi_c = _s5_power(lrc, lic, dtc, e_lane)

    for s in range(c):
        pr, pi = pr_r[c - 1 - s:c - s], pi_r[c - 1 - s:c - s]
        w_ref[0, s * LANES:(s + 1) * LANES, :] = jnp.concatenate(
            [bbt_re * pr - bbt_im * pi, bbt_re * pi + bbt_im * pr], axis=1).astype(BF16)

    for t in range(c):
        pr = jnp.broadcast_to(pr_c[:, t + 1:t + 2], (sp, LANES))
        pi = jnp.broadcast_to(pi_c[:, t + 1:t + 2], (sp, LANES))
        v_ref[0, :sp, t * LANES:(t + 1) * LANES] = (ct_re * pr - ct_im * pi).astype(BF16)
        v_ref[0, sp:, t * LANES:(t + 1) * LANES] = (-(ct_re * pi + ct_im * pr)).astype(BF16)

    vj_re = jnp.concatenate([c_re * pr_r[j:j + 1] - c_im * pi_r[j:j + 1] for j in range(c)], axis=0)
    vj_im = jnp.concatenate([c_re * pi_r[j:j + 1] + c_im * pr_r[j:j + 1] for j in range(c)], axis=0)
    kt = (jnp.dot(vj_re, bb_re, precision=HIGHEST, preferred_element_type=F32)
          - jnp.dot(vj_im, bb_im, precision=HIGHEST, preferred_element_type=F32))
    strip = jnp.concatenate([kt[j * LANES:(j + 1) * LANES].T for j in range(c)], axis=1)
    for s in range(c):
        blk = strip if s == 0 else jnp.concatenate(
            [jnp.zeros((LANES, s * LANES), F32), strip[:, :(c - s) * LANES]], axis=1)
        m_ref[0, s * LANES:(s + 1) * LANES, :] = blk.astype(BF16)

    are_ref[...] = pr_r[c:c + 1]
    aim_ref[...] = pi_r[c:c + 1]


def _s5_prep(lam_re, lam_im, log_dt, b_re, b_im, c_re, c_im):
    g, p = lam_re.shape
    gpb, hh = S5_GPB, SSM_GROUP
    blk3 = lambda d1, d2: pl.BlockSpec((gpb, d1, d2), lambda j: (j, 0, 0))
    out3 = lambda d1, d2: pl.BlockSpec((1, d1, d2), lambda j: (j, 0, 0))
    arow = pl.BlockSpec((1, gpb * p), lambda j: (0, j))
    return pl.pallas_call(
        _s5_prep_kernel,
        out_shape=(jax.ShapeDtypeStruct((S5_BLOCKS, S5_K, S5_K), BF16),
                   jax.ShapeDtypeStruct((S5_BLOCKS, S5_K, 2 * S5_HALF), BF16),
                   jax.ShapeDtypeStruct((S5_BLOCKS, 2 * S5_HALF, S5_K), BF16),
                   jax.ShapeDtypeStruct((1, g * p), F32), jax.ShapeDtypeStruct((1, g * p), F32)),
        grid=(S5_BLOCKS,),
        in_specs=[blk3(1, p), blk3(1, p), blk3(p, 1), blk3(p, 1), blk3(1, 1), blk3(1, 1),
                  blk3(hh, p), blk3(hh, p), blk3(p, hh), blk3(p, hh),
                  blk3(hh, p), blk3(hh, p), blk3(p, hh), blk3(p, hh)],
        out_specs=(out3(S5_K, S5_K), out3(S5_K, 2 * S5_HALF), out3(2 * S5_HALF, S5_K), arow, arow),
        compiler_params=_params(("parallel",)),
        name="s5_chunk_operators",
    )(lam_re.reshape(g, 1, p), lam_im.reshape(g, 1, p), lam_re.reshape(g, p, 1), lam_im.reshape(g, p, 1),
      log_dt.reshape(g, 1, 1), log_dt.reshape(g, 1, 1),
      jnp.swapaxes(b_re, 1, 2), jnp.swapaxes(b_im, 1, 2), b_re, b_im,
      c_re, c_im, jnp.swapaxes(c_re, 1, 2), jnp.swapaxes(c_im, 1, 2))


def _s5_gather_chunks(u_ref, rows):
    return jnp.concatenate([u_ref[pl.ds(s, rows, stride=S5_CHUNK), :] for s in range(S5_CHUNK)], axis=1)


def _s5_local_kernel(u_ref, w_ref, x_ref, dre_ref, dim_ref):
    rows = x_ref.shape[0]
    x = _s5_gather_chunks(u_ref, rows).astype(BF16)
    x_ref[...] = x
    d = jnp.dot(x, w_ref[0], preferred_element_type=F32)
    dre_ref[...] = d[:, :S5_HALF]
    dim_ref[...] = d[:, S5_HALF:]


def _s5_local(u2, w_blk, rows):
    t = u2.shape[0]
    nc = t // S5_CHUNK
    return pl.pallas_call(
        _s5_local_kernel,
        out_shape=(jax.ShapeDtypeStruct((nc, S5_BLOCKS * S5_K), BF16),
                   jax.ShapeDtypeStruct((nc, S5_BLOCKS * S5_HALF), F32),
                   jax.ShapeDtypeStruct((nc, S5_BLOCKS * S5_HALF), F32)),
        grid=(S5_BLOCKS, nc // rows),
        in_specs=[pl.BlockSpec((rows * S5_CHUNK, LANES), lambda j, i: (i, j)),
                  pl.BlockSpec((1, S5_K, 2 * S5_HALF), lambda j, i: (j, 0, 0))],
        out_specs=(pl.BlockSpec((rows, S5_K), lambda j, i: (i, j)),
                   pl.BlockSpec((rows, S5_HALF), lambda j, i: (i, j)),
                   pl.BlockSpec((rows, S5_HALF), lambda j, i: (i, j))),
        compiler_params=_params(("parallel", "parallel")),
        name="s5_chunk_state_increment",
    )(u2, w_blk)


def _s5_scan_kernel(dre_ref, dim_ref, h0re_ref, h0im_ref, ar_ref, ai_ref, hre_ref, him_ref, fre_ref, fim_ref,
                    cre_ref, cim_ref):
    @pl.when(pl.program_id(2) == 0)
    def _():
        cre_ref[...] = h0re_ref[0]
        cim_ref[...] = h0im_ref[0]

    ar = ar_ref[...]
    ai = ai_ref[...]

    def body(r, h):
        hr, hi = h
        hre_ref[0, pl.ds(r, 1), :] = hr
        him_ref[0, pl.ds(r, 1), :] = hi
        dr = dre_ref[0, pl.ds(r, 1), :]
        di = dim_ref[0, pl.ds(r, 1), :]
        return ar * hr - ai * hi + dr, ar * hi + ai * hr + di

    hr, hi = lax.fori_loop(0, dre_ref.shape[1], body, (cre_ref[...], cim_ref[...]))
    cre_ref[...] = hr
    cim_ref[...] = hi
    fre_ref[0] = hr
    fim_ref[0] = hi


def _s5_scan(d_re, d_im, h0_re, h0_im, a_re, a_im, tc, tw):
    b, ncb, w = d_re.shape
    dspec = pl.BlockSpec((1, tc, tw), lambda i, k, c: (i, c, k))
    hspec = pl.BlockSpec((1, 1, tw), lambda i, k, c: (i, 0, k))
    aspec = pl.BlockSpec((1, tw), lambda i, k, c: (0, k))
    big = jax.ShapeDtypeStruct((b, ncb, w), F32)
    small = jax.ShapeDtypeStruct((b, 1, w), F32)
    return pl.pallas_call(
        _s5_scan_kernel,
        out_shape=(big, big, small, small),
        grid=(b, w // tw, ncb // tc),
        in_specs=[dspec, dspec, hspec, hspec, aspec, aspec],
        out_specs=(dspec, dspec, hspec, hspec),
        scratch_shapes=[pltpu.VMEM((1, tw), F32), pltpu.VMEM((1, tw), F32)],
        compiler_params=_params(("parallel", "parallel", "arbitrary")),
        name="s5_chunk_state_scan",
    )(d_re, d_im, h0_re, h0_im, a_re, a_im)


def _gelu_tanh(x):
    return x * (0.5 * (1.0 + jnp.tanh(math.sqrt(2.0 / math.pi) * (x + 0.044715 * (x * x * x)))))


def _s5_out_kernel(x_ref, hre_ref, him_ref, m_ref, v_ref, u_ref, d_ref, z_ref, z_scr):
    rows = x_ref.shape[0]
    hre = hre_ref[...].astype(BF16)
    him = him_ref[...].astype(BF16)
    skip = d_ref[...]
    pair = 2 * LANES
    for q in range(S5_K // pair):
        cols = slice(q * pair, (q + 1) * pair)
        k = (q + 1) * pair
        y = (jnp.dot(x_ref[:, :k], m_ref[0, :k, cols], preferred_element_type=F32)
             + jnp.dot(hre, v_ref[0, :S5_HALF, cols], preferred_element_type=F32)
             + jnp.dot(him, v_ref[0, S5_HALF:, cols], preferred_element_type=F32))
        for e in range(2):
            tok = pl.ds(2 * q + e, rows, stride=S5_CHUNK)
            z_scr[tok, :] = _gelu_tanh(y[:, e * LANES:(e + 1) * LANES] + skip * u_ref[tok, :])
    z_ref[...] = z_scr[...].astype(BF16)


def _s5_out(x, h_re, h_im, m_blk, v_blk, u2, ssm_d, rows):
    nc = x.shape[0]
    t = nc * S5_CHUNK
    return pl.pallas_call(
        _s5_out_kernel,
        out_shape=jax.ShapeDtypeStruct((t, SSM_WIDTH), BF16),
        grid=(S5_BLOCKS, nc // rows),
        in_specs=[pl.BlockSpec((rows, S5_K), lambda j, i: (i, j)),
                  pl.BlockSpec((rows, S5_HALF), lambda j, i: (i, j)),
                  pl.BlockSpec((rows, S5_HALF), lambda j, i: (i, j)),
                  pl.BlockSpec((1, S5_K, S5_K), lambda j, i: (j, 0, 0)),
                  pl.BlockSpec((1, 2 * S5_HALF, S5_K), lambda j, i: (j, 0, 0)),
                  pl.BlockSpec((rows * S5_CHUNK, LANES), lambda j, i: (i, j)),
                  pl.BlockSpec((1, LANES), lambda j, i: (0, j))],
        out_specs=pl.BlockSpec((rows * S5_CHUNK, LANES), lambda j, i: (i, j)),
        scratch_shapes=[pltpu.VMEM((rows * S5_CHUNK, LANES), F32)],
        compiler_params=_params(("parallel", "parallel")),
        name="s5_chunk_output",
    )(x, h_re, h_im, m_blk, v_blk, u2, ssm_d.reshape(1, SSM_WIDTH))


def _state_to_blocks(h):
    return h.astype(F32).reshape(h.shape[0], 1, N_GROUPS * SSM_STATE)


def _s5(u, h0_re, h0_im, blk_ops, ssm_d):
    m_blk, w_blk, v_blk, a_re, a_im = blk_ops
    b, l, _ = u.shape
    ncb = l // S5_CHUNK
    w = N_GROUPS * SSM_STATE
    u2 = u.reshape(b * l, SSM_WIDTH)
    rows = min(b * ncb, 256)
    x, d_re, d_im = _s5_local(u2, w_blk, rows)
    if h0_re is None:
        h0_re = h0_im = jnp.zeros((b, N_GROUPS, SSM_STATE), F32)
    hs_re, hs_im, hf_re, hf_im = _s5_scan(d_re.reshape(b, ncb, w), d_im.reshape(b, ncb, w),
                                          _state_to_blocks(h0_re), _state_to_blocks(h0_im), a_re, a_im,
                                          tc=min(ncb, 64), tw=1024)
    z = _s5_out(x, hs_re.reshape(b * ncb, w), hs_im.reshape(b * ncb, w), m_blk, v_blk, u2, ssm_d, rows)
    return (z.reshape(b, l, SSM_WIDTH), hf_re.reshape(b, N_GROUPS, SSM_STATE), hf_im.reshape(b, N_GROUPS, SSM_STATE))


def _mix_kernel(x_ref, z_ref, o_ref, gate_ref, wg_ref, bg_ref, wo_ref, out_ref):
    nb, tl, d = x_ref.shape
    rows = nb * tl
    gl = jnp.dot(z_ref[...].reshape(rows, SSM_WIDTH), wg_ref[...], preferred_element_type=F32) + bg_ref[...]
    yg = (gl[:, :SSM_WIDTH] * _sigmoid(gl[:, SSM_WIDTH:])).astype(BF16)
    mix = (jnp.dot(o_ref[...].reshape(rows, ATTN_WIDTH), wo_ref[:ATTN_WIDTH, :], preferred_element_type=F32)
           + jnp.dot(yg, wo_ref[ATTN_WIDTH:, :], preferred_element_type=F32))
    out_ref[...] = x_ref[...] + gate_ref[...] * mix.reshape(nb, tl, d)


def _mix(x, z_ssm, o_attn, gate, w_glu_b, b_glu, w_out_b, nb, tl):
    b, l, d = x.shape
    xspec = lambda w: pl.BlockSpec((nb, tl, w), lambda i, j: (i, j, 0))
    const = lambda shape: pl.BlockSpec(shape, lambda i, j: (0, 0), pipeline_mode=pl.Buffered(1))
    return pl.pallas_call(
        _mix_kernel,
        out_shape=jax.ShapeDtypeStruct((b, l, d), F32),
        grid=(b // nb, l // tl),
        in_specs=[xspec(d), xspec(SSM_WIDTH), xspec(ATTN_WIDTH),
                  pl.BlockSpec((nb, 1, d), lambda i, j: (i, 0, 0)),
                  const((SSM_WIDTH, 2 * SSM_WIDTH)), const((1, 2 * SSM_WIDTH)), const((d, d))],
        out_specs=xspec(d),
        compiler_params=_params(("parallel", "parallel")),
        name="glu_outproj_residual",
    )(x, z_ssm, o_attn, gate, w_glu_b, b_glu.reshape(1, -1), w_out_b)


FFN_ROWS = 512
FFN_TF = 512

def _ffn_kernel(x_ref, sc_ref, sh_ref, gate_ref, g_ref, gf_ref, w1_ref, w2_ref, y_ref, h_ref, acc_ref):
    nb, tl, d = x_ref.shape
    j = pl.program_id(2)

    @pl.when(j == 0)
    def _():
        h = _modulated_norm(x_ref[...], g_ref[...], sc_ref[...], sh_ref[...])
        h_ref[...] = h.reshape(nb * tl, d).astype(BF16)
        acc_ref[...] = jnp.zeros_like(acc_ref)

    hid = jnp.dot(h_ref[...], w1_ref[...], preferred_element_type=F32)
    hid = jnp.square(jnp.maximum(hid, 0.0)).astype(BF16)
    acc_ref[...] += jnp.dot(hid, w2_ref[...], preferred_element_type=F32)

    @pl.when(j == pl.num_programs(2) - 1)
    def _():
        x = x_ref[...] + gate_ref[...] * acc_ref[...].reshape(nb, tl, d)
        ms = jnp.mean(x * x, axis=-1, keepdims=True)
        y_ref[...] = x * lax.rsqrt(ms + EPS) * gf_ref[...]


def _ffn(x, scale, shift, gate, g_ffn, g_final, w1_b, w2_b, nb, tl, tf):
    b, l, d = x.shape
    xspec = pl.BlockSpec((nb, tl, d), lambda i, j, k: (i, j, 0))
    mspec = pl.BlockSpec((nb, 1, d), lambda i, j, k: (i, 0, 0))
    gspec = pl.BlockSpec((1, d), lambda i, j, k: (0, 0))
    return pl.pallas_call(
        _ffn_kernel,
        out_shape=jax.ShapeDtypeStruct((b, l, d), F32),
        grid=(b // nb, l // tl, D_FF // tf),
        in_specs=[xspec, mspec, mspec, mspec, gspec, gspec,
                  pl.BlockSpec((d, tf), lambda i, j, k: (0, k)),
                  pl.BlockSpec((tf, d), lambda i, j, k: (k, 0))],
        out_specs=xspec,
        scratch_shapes=[pltpu.VMEM((nb * tl, d), BF16), pltpu.VMEM((nb * tl, d), F32)],
        compiler_params=_params(("parallel", "parallel", "arbitrary")),
        name="relu2_mlp_final_norm",
    )(x, scale, shift, gate, g_ffn.reshape(1, d), g_final.reshape(1, d), w1_b, w2_b)


def _layer(x, mods, tables, tables_t, past_kt, past_v2, h0_re, h0_im, wts, nb, tl):
    shift1, scale1, gate1, shift2, scale2, gate2 = mods
    b, l, _ = x.shape
    first_chunk = past_kt is None
    w_k = wts["w_k_t"] if first_chunk else wts["w_k"]
    q, k, v, u, kb, vb = _inproj(x, scale1, shift1, wts["g_mix"], wts["w_qvu"], w_k, tables, tables_t, nb, tl)
    if first_chunk:
        o = _attn_prompt(q, kb, vb, wts["lam_vecs"], wts["g_subln"])
        k6 = k.reshape(b, N_HEADS, 2, HEAD_DIM, l).transpose(0, 4, 1, 2, 3)[None]
    else:
        o = _attn_sample(q, past_kt, past_v2, kb, vb, wts["lam_vecs"], wts["g_subln"])
        k6 = k.reshape(1, b, l, N_HEADS, 2, HEAD_DIM)
    z_ssm, h_re, h_im = _s5(u, h0_re, h0_im, wts["s5_ops"], wts["ssm_d"])
    x1 = _mix(x, z_ssm, o, gate1, wts["w_glu"], wts["b_glu"], wts["w_out"], nb, tl)
    tl_ffn = min(l, FFN_ROWS) if nb == 1 else tl
    y = _ffn(x1, scale2, shift2, gate2, wts["g_ffn"], wts["g_final"], wts["w_ff1"], wts["w_ff2"],
             nb, tl_ffn, tf=FFN_TF)
    v5 = v.reshape(1, b, l, N_HEADS, V_DIM)
    return y, k6, v5, h_re[None], h_im[None]


def kernel(x_prompt, x_sample, c_prompt, c_sample, cache_k, cache_v, state_ssm_re, state_ssm_im, w_ada, b_ada, g_mix, w_in, lam_q1, lam_k1, lam_q2, lam_k2, g_subln, ssm_lam_re, ssm_lam_im, ssm_log_dt, ssm_b_re, ssm_b_im, ssm_c_re, ssm_c_im, ssm_d, w_glu, b_glu, w_out, g_ffn, w_ff1, w_ff2, g_final):
    bp, lp, d = x_prompt.shape
    bs, ls, _ = x_sample.shape
    past = cache_k.shape[2]

    m = _ada(jnp.concatenate([c_prompt, c_sample], axis=0), w_ada[0], b_ada[0])
    mods = [m[:, i * d:(i + 1) * d].reshape(bp + bs, 1, d) for i in range(6)]
    mods_p = [t[:bp] for t in mods]
    mods_s = [t[bp:] for t in mods]

    w_in_b = w_in[0].astype(BF16)
    w_k = w_in_b[:, ATTN_WIDTH:2 * ATTN_WIDTH]
    wts = {
        "g_mix": g_mix[0],
        "w_qvu": jnp.concatenate([w_in_b[:, :ATTN_WIDTH], w_in_b[:, 2 * ATTN_WIDTH:]], axis=1),
        "w_k": w_k, "w_k_t": w_k.T,
        "lam_vecs": tuple(t[0].reshape(1, HEAD_DIM) for t in (lam_q1, lam_k1, lam_q2, lam_k2)),
        "g_subln": g_subln[0],
        "s5_ops": _s5_prep(ssm_lam_re[0], ssm_lam_im[0], ssm_log_dt[0], ssm_b_re[0], ssm_b_im[0],
                           ssm_c_re[0], ssm_c_im[0]),
        "ssm_d": ssm_d[0], "w_glu": w_glu[0].astype(BF16), "b_glu": b_glu[0],
        "w_out": w_out[0].astype(BF16), "g_ffn": g_ffn[0], "g_final": g_final,
        "w_ff1": w_ff1[0].astype(BF16), "w_ff2": w_ff2[0].astype(BF16),
    }

    tl_p = min(lp, 512)
    yp, kp, vp, srp, sip = _layer(x_prompt, mods_p, _rope_tables(lp, 0), _rope_tables_t(lp, 0),
                                  None, None, None, None, wts, nb=1, tl=tl_p)
    nb_s = max(1, min(bs, 512 // ls))
    ckt = cache_k[0].reshape(bs, past, ATTN_WIDTH).transpose(0, 2, 1)
    cv2 = cache_v[0].reshape(bs, past * N_HEADS, V_DIM)
    ys, ks, vs, srs, sis = _layer(x_sample, mods_s, _rope_tables(ls, past), None, ckt, cv2,
                                  state_ssm_re[0], state_ssm_im[0], wts, nb=nb_s, tl=ls)
    return (yp, ys, kp, vp, srp, sip, ks, vs, srs, sis)
```

```python
import functools
import math

import jax
import jax.numpy as jnp
from jax import lax
from jax.experimental import pallas as pl
from jax.experimental.pallas import tpu as pltpu

F32 = jnp.float32
BF16 = jnp.bfloat16

D_MODEL = 2048
ATTN_WIDTH = D_MODEL // 2
SSM_WIDTH = D_MODEL - ATTN_WIDTH
N_HEADS = 8
HEAD_DIM = ATTN_WIDTH // (2 * N_HEADS)
V_DIM = 2 * HEAD_DIM
ROT_DIM = HEAD_DIM // 4
ROPE_THETA = 500000.0
CHUNK = 64
SSM_GROUP = 16
N_GROUPS = SSM_WIDTH // SSM_GROUP
SSM_STATE = 64
IN_WIDTH = 3 * ATTN_WIDTH + SSM_WIDTH
D_FF = 4 * D_MODEL
EPS = 1e-6
NEG_INF = -1e30
LAM_INIT = 0.8 - 0.6 * math.exp(-0.3 * 0)

LANES = 128
S5_CHUNK = 16
S5_BLOCKS = SSM_WIDTH // LANES
S5_GPB = LANES // SSM_GROUP
S5_K = S5_CHUNK * LANES
S5_HALF = S5_GPB * SSM_STATE
VMEM_LIMIT = 56 * 1024 * 1024

HIGHEST = lax.Precision.HIGHEST


def _params(sem, vmem=VMEM_LIMIT):
    return pltpu.CompilerParams(dimension_semantics=sem, vmem_limit_bytes=vmem)


def _sigmoid(x):
    return 1.0 / (1.0 + jnp.exp(-x))


def _ada_kernel(c_ref, w_ref, b_ref, o_ref):
    c = c_ref[...]
    s = (c * _sigmoid(c)).astype(BF16)
    o_ref[...] = jnp.dot(s, w_ref[...].astype(BF16), preferred_element_type=F32) + b_ref[...]


def _ada(c_all, w_ada, b_ada):
    nb, d = c_all.shape
    n = w_ada.shape[1]
    tn = 1024
    return pl.pallas_call(
        _ada_kernel,
        out_shape=jax.ShapeDtypeStruct((nb, n), F32),
        grid=(n // tn,),
        in_specs=[pl.BlockSpec((nb, d), lambda j: (0, 0)),
                  pl.BlockSpec((d, tn), lambda j: (0, j)),
                  pl.BlockSpec((1, tn), lambda j: (0, j))],
        out_specs=pl.BlockSpec((nb, tn), lambda j: (0, j)),
        compiler_params=_params(("parallel",)),
        name="ada_modulation",
    )(c_all, w_ada, b_ada.reshape(1, n))


def _rope_angles(pos, freq_index):
    inv = jnp.exp(freq_index * (-2.0 * math.log(ROPE_THETA) / ROT_DIM))
    return pos * inv


def _rope_kernel(c_ref, sa_ref, sb_ref, *, offset):
    tl = c_ref.shape[0]
    row = lax.broadcasted_iota(jnp.int32, (tl, LANES), 0)
    lane = lax.broadcasted_iota(jnp.int32, (tl, LANES), 1)
    pos = (offset + pl.program_id(0) * tl + row).astype(F32)
    d = lane % HEAD_DIM
    ang = _rope_angles(pos, (d % (ROT_DIM // 2)).astype(F32))
    c = jnp.cos(ang)
    s = jnp.sin(ang)
    in_rot = d < ROT_DIM
    first = d < ROT_DIM // 2
    c_ref[...] = jnp.where(in_rot, c, 1.0)
    sa_ref[...] = jnp.where(in_rot, jnp.where(first, 0.0, s), 0.0)
    sb_ref[...] = jnp.where(first, -s, 0.0)


def _rope_tables(length, offset):
    tl = min(length, 1024)
    shp = jax.ShapeDtypeStruct((length, LANES), F32)
    spec = pl.BlockSpec((tl, LANES), lambda i: (i, 0))
    return pl.pallas_call(
        functools.partial(_rope_kernel, offset=offset),
        out_shape=(shp, shp, shp),
        grid=(length // tl,),
        out_specs=(spec, spec, spec),
        compiler_params=_params(("parallel",)),
        name="rope_tables",
    )()


def _rope_t_kernel(c_ref, s_ref, *, offset):
    n = c_ref.shape[1]
    fi = lax.broadcasted_iota(jnp.int32, (ROT_DIM // 2, n), 0).astype(F32)
    pos = (offset + pl.program_id(0) * n + lax.broadcasted_iota(jnp.int32, (ROT_DIM // 2, n), 1)).astype(F32)
    ang = _rope_angles(pos, fi)
    c_ref[...] = jnp.cos(ang)
    s_ref[...] = jnp.sin(ang)


def _rope_tables_t(length, offset):
    tn = min(length, 2048)
    shp = jax.ShapeDtypeStruct((ROT_DIM // 2, length), F32)
    spec = pl.BlockSpec((ROT_DIM // 2, tn), lambda i: (0, i))
    return pl.pallas_call(
        functools.partial(_rope_t_kernel, offset=offset),
        out_shape=(shp, shp),
        grid=(length // tn,),
        out_specs=(spec, spec),
        compiler_params=_params(("parallel",)),
        name="rope_tables_transposed",
    )()


Q_SCALE = HEAD_DIM ** -0.5 * math.log2(math.e)


def _modulated_norm(x, g, scale, shift):
    ms = jnp.mean(x * x, axis=-1, keepdims=True)
    y = x * lax.rsqrt(ms + EPS) * g
    return y * (1.0 + scale) + shift


def _inproj_kernel(*refs, k_t):
    if k_t:
        (x_ref, sc_ref, sh_ref, g_ref, w_ref, wk_ref, c_ref, sa_ref, sb_ref, ct_ref, st_ref,
         q_ref, k_ref, v_ref, u_ref, kb_ref, vb_ref) = refs
    else:
        (x_ref, sc_ref, sh_ref, g_ref, w_ref, wk_ref, c_ref, sa_ref, sb_ref,
         q_ref, k_ref, v_ref, u_ref, kb_ref, vb_ref) = refs
    nb, tl, d = x_ref.shape
    rows = nb * tl
    h = _modulated_norm(x_ref[...], g_ref[...], sc_ref[...], sh_ref[...])
    hb = h.reshape(rows, d).astype(BF16)
    cos = c_ref[...]
    sin_a = sa_ref[...]
    sin_b = sb_ref[...]

    def rope(t):
        ra = pltpu.roll(t, ROT_DIM // 2, 1).reshape(nb, tl, LANES)
        rb = pltpu.roll(t, LANES - ROT_DIM // 2, 1).reshape(nb, tl, LANES)
        return t.reshape(nb, tl, LANES) * cos + ra * sin_a + rb * sin_b

    q = jnp.dot(hb, w_ref[:, 0:ATTN_WIDTH], preferred_element_type=F32)
    for hd in range(N_HEADS):
        sl = slice(hd * LANES, (hd + 1) * LANES)
        q_ref[:, :, sl] = (rope(q[:, sl]) * Q_SCALE).astype(BF16)
    if k_t:
        kt = _nt_dot(wk_ref[...], hb)
        cos_t = ct_ref[...]
        sin_t = st_ref[...]
        half = ROT_DIM // 2
        for comp in range(2 * N_HEADS):
            b0 = comp * HEAD_DIM
            t1 = kt[b0:b0 + half]
            t2 = kt[b0 + half:b0 + ROT_DIM]
            blk = jnp.concatenate([t1 * cos_t - t2 * sin_t, t2 * cos_t + t1 * sin_t,
                                   kt[b0 + ROT_DIM:b0 + HEAD_DIM]], axis=0)
            k_ref[0, b0:b0 + HEAD_DIM, :] = blk
            kb_ref[0, b0:b0 + HEAD_DIM, :] = blk.astype(BF16)
    else:
        k = jnp.dot(hb, wk_ref[...], preferred_element_type=F32)
        for hd in range(N_HEADS):
            sl = slice(hd * LANES, (hd + 1) * LANES)
            kr = rope(k[:, sl])
            k_ref[:, :, sl] = kr
            kb_ref[:, :, sl] = kr.astype(BF16)
    v = jnp.dot(hb, w_ref[:, ATTN_WIDTH:2 * ATTN_WIDTH], preferred_element_type=F32)
    v = v.reshape(nb, tl, ATTN_WIDTH)
    v_ref[...] = v
    vb_ref[...] = v.astype(BF16)
    u = jnp.dot(hb, w_ref[:, 2 * ATTN_WIDTH:], preferred_element_type=F32)
    u_ref[...] = u.reshape(nb, tl, SSM_WIDTH)


def _inproj(x, scale, shift, g_mix, w_qvu, w_k, tables, tables_t, nb, tl):
    b, l, d = x.shape
    k_t = tables_t is not None
    assert not k_t or nb == 1
    grid = (b // nb, l // tl)
    xspec = lambda w: pl.BlockSpec((nb, tl, w), lambda i, j: (i, j, 0))
    mspec = pl.BlockSpec((nb, 1, d), lambda i, j: (i, 0, 0))
    tspec = pl.BlockSpec((tl, LANES), lambda i, j: (j, 0))
    const = lambda a: pl.BlockSpec(a.shape, lambda i, j: (0, 0), pipeline_mode=pl.Buffered(1))
    out = lambda dt: jax.ShapeDtypeStruct((b, l, ATTN_WIDTH), dt)
    in_specs = [xspec(d), mspec, mspec, pl.BlockSpec((1, d), lambda i, j: (0, 0)),
                const(w_qvu), const(w_k), tspec, tspec, tspec]
    args = [x, scale, shift, g_mix.reshape(1, d), w_qvu, w_k, *tables]
    kspec, kshape = xspec(ATTN_WIDTH), out
    if k_t:
        in_specs += [pl.BlockSpec((ROT_DIM // 2, tl), lambda i, j: (0, j))] * 2
        args += list(tables_t)
        kspec = pl.BlockSpec((1, ATTN_WIDTH, tl), lambda i, j: (i, 0, j))
        kshape = lambda dt: jax.ShapeDtypeStruct((b, ATTN_WIDTH, l), dt)
    return pl.pallas_call(
        functools.partial(_inproj_kernel, k_t=k_t),
        out_shape=(out(BF16), kshape(F32), out(F32), out(F32), kshape(BF16), out(BF16)),
        grid=grid,
        in_specs=in_specs,
        out_specs=(xspec(ATTN_WIDTH), kspec, xspec(ATTN_WIDTH), xspec(ATTN_WIDTH), kspec, xspec(ATTN_WIDTH)),
        compiler_params=_params(("parallel", "parallel")),
        name="norm_inproj_rope",
    )(*args)


ATTN_TQ = 512
ATTN_TK = 512
ATTN_NSUB = 4


def _stack_components(q):
    lane = lax.broadcasted_iota(jnp.int32, q.shape, 1)
    zero = jnp.zeros_like(q)
    return jnp.concatenate([jnp.where(lane < HEAD_DIM, q, zero),
                            jnp.where(lane >= HEAD_DIM, q, zero)], axis=0)


def _diff_epilogue(num, den, tq, lq1, lk1, lq2, lk2, g_subln):
    lam = (jnp.exp(jnp.sum(lq1 * lk1, axis=-1, keepdims=True))
           - jnp.exp(jnp.sum(lq2 * lk2, axis=-1, keepdims=True)) + LAM_INIT)
    on = num / den
    o = on[:tq] - lam * on[tq:]
    ms = jnp.mean(o * o, axis=-1, keepdims=True)
    return o * lax.rsqrt(ms + EPS) * g_subln * (1.0 - LAM_INIT)


def _nt_dot(a, b):
    return lax.dot_general(a, b, (((1,), (1,)), ((), ())), preferred_element_type=F32)


def _attn_prompt_kernel(lq1_ref, lk1_ref, lq2_ref, lk2_ref, gs_ref, q_ref, kt_ref, v_ref, o_ref,
                        qs_scr, s_scr, p_scr, m_scr, acc_scr, *, tq, tk, nsub):
    i = pl.program_id(2)
    rows = 2 * tq
    qs_scr[...] = _stack_components(q_ref[0])
    span = nsub * tk
    nfull = (i * tq) // span
    ones = jnp.ones((tk, LANES), BF16)
    m_scr[...] = jnp.full((rows, LANES), NEG_INF, F32)
    acc_scr[...] = jnp.zeros((rows, 2 * LANES), F32)

    def scores(start, u, masked):
        s = jnp.dot(qs_scr[...], kt_ref[0, :, pl.ds(start, tk)], preferred_element_type=F32)
        if masked:
            r = lax.broadcasted_iota(jnp.int32, s.shape, 0)
            c = lax.broadcasted_iota(jnp.int32, s.shape, 1)
            qc = (i * tq + jnp.where(r >= tq, r - tq, r)) // CHUNK
            s = jnp.where((start + c) // CHUNK <= qc, s, NEG_INF)
        s_scr[u] = s
        return jnp.broadcast_to(jnp.max(s, axis=-1, keepdims=True), (rows, LANES))

    def accumulate(start, u, block_max):
        m_old = m_scr[...]
        m_new = jnp.maximum(m_old, block_max)
        alpha = jnp.exp2(m_old - m_new)
        m_scr[...] = m_new
        p = jnp.exp2(s_scr[u] - jnp.concatenate([m_new] * (tk // LANES), axis=1))
        p_scr[u] = p.astype(BF16)
        v1 = jnp.concatenate([v_ref[0, pl.ds(start, tk), :], ones], axis=1)
        pv = jnp.dot(p_scr[u], v1, preferred_element_type=F32)
        acc_scr[...] = jnp.concatenate([alpha, alpha], axis=1) * acc_scr[...] + pv

    def body(t, _):
        base = pl.multiple_of(t * span, span)
        maxes = [scores(base + u * tk, u, False) for u in range(nsub)]
        for u in range(nsub):
            accumulate(base + u * tk, u, maxes[u])
        return 0
    lax.fori_loop(0, nfull, body, 0)

    left = i - nfull * nsub
    for n_left in range(nsub):
        @pl.when(left == n_left)
        def _(n_left=n_left):
            base = pl.multiple_of(nfull * span, span)
            maxes = [scores(base + u * tk, u, u == n_left) for u in range(n_left + 1)]
            for u in range(n_left + 1):
                accumulate(base + u * tk, u, maxes[u])

    acc = acc_scr[...]
    o = _diff_epilogue(acc[:, :LANES], acc[:, LANES:], tq, lq1_ref[...], lk1_ref[...], lq2_ref[...],
                       lk2_ref[...], gs_ref[...])
    o_ref[0] = o.astype(BF16)


def _attn_prompt(q, ktb, vb, lam_vecs, g_subln):
    b, l, _ = q.shape
    tq = min(l, ATTN_TQ)
    tk = min(l, ATTN_TK)
    nsub = ATTN_NSUB
    assert l % tq == 0 and tq == tk and tq % CHUNK == 0
    vec = pl.BlockSpec((1, HEAD_DIM), lambda bi, h, i: (0, 0))
    return pl.pallas_call(
        functools.partial(_attn_prompt_kernel, tq=tq, tk=tk, nsub=nsub),
        out_shape=jax.ShapeDtypeStruct((b, l, ATTN_WIDTH), BF16),
        grid=(b, N_HEADS, l // tq),
        in_specs=[vec, vec, vec, vec,
                  pl.BlockSpec((1, V_DIM), lambda bi, h, i: (0, 0)),
                  pl.BlockSpec((1, tq, LANES), lambda bi, h, i: (bi, i, h)),
                  pl.BlockSpec((1, LANES, l), lambda bi, h, i: (bi, h, 0)),
                  pl.BlockSpec((1, l, LANES), lambda bi, h, i: (bi, 0, h))],
        out_specs=pl.BlockSpec((1, tq, LANES), lambda bi, h, i: (bi, i, h)),
        scratch_shapes=[pltpu.VMEM((2 * tq, LANES), BF16), pltpu.VMEM((nsub, 2 * tq, tk), F32),
                        pltpu.VMEM((nsub, 2 * tq, tk), BF16),
                        pltpu.VMEM((2 * tq, LANES), F32), pltpu.VMEM((2 * tq, 2 * LANES), F32)],
        compiler_params=_params(("parallel", "parallel", "arbitrary")),
        name="prompt_diff_attention",
    )(*lam_vecs, g_subln.reshape(1, V_DIM), q, ktb, vb)


def _attn_sample_kernel(lq1_ref, lk1_ref, lq2_ref, lk2_ref, gs_ref, q_ref, ckt_ref, cv_ref,
                        kn_ref, vn_ref, o_ref):
    tq = q_ref.shape[1]
    past = ckt_ref.shape[2]
    for hd in range(N_HEADS):
        cols = slice(hd * LANES, (hd + 1) * LANES)
        qs = _stack_components(q_ref[0, :, cols])
        s_p = jnp.dot(qs, ckt_ref[0, cols, :].astype(BF16), preferred_element_type=F32)
        s_n = _nt_dot(qs, kn_ref[0, :, cols])
        m = jnp.maximum(jnp.max(s_p, axis=-1, keepdims=True), jnp.max(s_n, axis=-1, keepdims=True))
        p_p = jnp.exp2(s_p - m)
        p_n = jnp.exp2(s_n - m)
        l = jnp.sum(p_p, axis=-1, keepdims=True) + jnp.sum(p_n, axis=-1, keepdims=True)
        vp = cv_ref[0, pl.ds(hd, past, stride=N_HEADS), :].astype(BF16)
        acc = (jnp.dot(p_p.astype(BF16), vp, preferred_element_type=F32)
               + jnp.dot(p_n.astype(BF16), vn_ref[0, :, cols], preferred_element_type=F32))
        o = _diff_epilogue(acc, l, tq, lq1_ref[...], lk1_ref[...], lq2_ref[...], lk2_ref[...], gs_ref[...])
        o_ref[0, :, cols] = o.astype(BF16)


def _attn_sample(q, cache_kt, cache_v2, kb, vb, lam_vecs, g_subln):
    b, l, _ = q.shape
    past = cache_kt.shape[2]
    vec = pl.BlockSpec((1, HEAD_DIM), lambda bi: (0, 0))
    new = pl.BlockSpec((1, l, ATTN_WIDTH), lambda bi: (bi, 0, 0))
    return pl.pallas_call(
        _attn_sample_kernel,
        out_shape=jax.ShapeDtypeStruct((b, l, ATTN_WIDTH), BF16),
        grid=(b,),
        in_specs=[vec, vec, vec, vec, pl.BlockSpec((1, V_DIM), lambda bi: (0, 0)),
                  new,
                  pl.BlockSpec((1, ATTN_WIDTH, past), lambda bi: (bi, 0, 0)),
                  pl.BlockSpec((1, past * N_HEADS, LANES), lambda bi: (bi, 0, 0)),
                  new, new],
        out_specs=new,
        compiler_params=_params(("parallel",)),
        name="sample_diff_attention",
    )(*lam_vecs, g_subln.reshape(1, V_DIM), q, cache_kt, cache_v2, kb, vb)


def _s5_discretize(lr, li, dt):
    mag = jnp.exp(lr * dt)
    ar = mag * jnp.cos(li * dt)
    ai = mag * jnp.sin(li * dt)
    den = lr * lr + li * li
    f_re = ((ar - 1.0) * lr + ai * li) / den
    f_im = (ai * lr - (ar - 1.0) * li) / den
    return f_re, f_im


def _s5_power(lr, li, dt, e):
    mag = jnp.exp(lr * dt * e)
    ang = li * dt * e
    return mag * jnp.cos(ang), mag * jnp.sin(ang)


def _place(x, offset, width):
    c = x.shape[1]
    row = lax.broadcasted_iota(jnp.int32, (c, width), 0)
    lane = lax.broadcasted_iota(jnp.int32, (c, width), 1)
    return jnp.dot(x, (lane == row + offset).astype(F32), precision=HIGHEST, preferred_element_type=F32)


def _s5_prep_kernel(lrr_ref, lir_ref, lrc_ref, lic_ref, ldtr_ref, ldtc_ref, brt_ref, bit_ref, br_ref, bi_ref,
                    cr_ref, ci_ref, crt_ref, cit_ref, m_ref, w_ref, v_ref, are_ref, aim_ref):
    gpb, p, hh, c = S5_GPB, SSM_STATE, SSM_GROUP, S5_CHUNK
    sp = gpb * p

    lrr = jnp.concatenate([lrr_ref[g] for g in range(gpb)], axis=1)
    lir = jnp.concatenate([lir_ref[g] for g in range(gpb)], axis=1)
    dtr = jnp.exp(jnp.concatenate([jnp.broadcast_to(ldtr_ref[g], (1, p)) for g in range(gpb)], axis=1))
    lrc = lrc_ref[...].reshape(sp, 1)
    lic = lic_ref[...].reshape(sp, 1)
    dtc = jnp.exp(jnp.broadcast_to(ldtc_ref[...], (gpb, p, 1)).reshape(sp, 1))
    f_re_r, f_im_r = _s5_discretize(lrr, lir, dtr)
    f_re_c, f_im_c = _s5_discretize(lrc, lic, dtc)

    bt_re = jnp.concatenate([_place(brt_ref[g], g * p, sp) for g in range(gpb)], axis=0)
    bt_im = jnp.concatenate([_place(bit_ref[g], g * p, sp) for g in range(gpb)], axis=0)
    b_re = jnp.concatenate([_place(br_ref[g], g * hh, LANES) for g in range(gpb)], axis=0)
    b_im = jnp.concatenate([_place(bi_ref[g], g * hh, LANES) for g in range(gpb)], axis=0)
    c_re = jnp.concatenate([_place(cr_ref[g], g * p, sp) for g in range(gpb)], axis=0)
    c_im = jnp.concatenate([_place(ci_ref[g], g * p, sp) for g in range(gpb)], axis=0)
    ct_re = jnp.concatenate([_place(crt_ref[g], g * hh, LANES) for g in range(gpb)], axis=0)
    ct_im = jnp.concatenate([_place(cit_ref[g], g * hh, LANES) for g in range(gpb)], axis=0)
    bbt_re = f_re_r * bt_re - f_im_r * bt_im
    bbt_im = f_re_r * bt_im + f_im_r * bt_re
    bb_re = f_re_c * b_re - f_im_c * b_im
    bb_im = f_re_c * b_im + f_im_c * b_re

    e_row = lax.broadcasted_iota(jnp.int32, (c + 1, 1), 0).astype(F32)
    pr_r, pi_r = _s5_power(lrr, lir, dtr, e_row)
    e_lane = lax.broadcasted_iota(jnp.int32, (1, LANES), 1)
    e_lane = jnp.where(e_lane <= c, e_lane, 0).astype(F32)
    pr_c, pi_c = _s5_power(lrc, lic, dtc, e_lane)

    for s in range(c):
        pr, pi = pr_r[c - 1 - s:c - s], pi_r[c - 1 - s:c - s]
        w_ref[0, s * LANES:(s + 1) * LANES, :] = jnp.concatenate(
            [bbt_re * pr - bbt_im * pi, bbt_re * pi + bbt_im * pr], axis=1).astype(BF16)

    for t in range(c):
        pr = jnp.broadcast_to(pr_c[:, t + 1:t + 2], (sp, LANES))
        pi = jnp.broadcast_to(pi_c[:, t + 1:t + 2], (sp, LANES))
        v_ref[0, :sp, t * LANES:(t + 1) * LANES] = (ct_re * pr - ct_im * pi).astype(BF16)
        v_ref[0, sp:, t * LANES:(t + 1) * LANES] = (-(ct_re * pi + ct_im * pr)).astype(BF16)

    vj_re = jnp.concatenate([c_re * pr_r[j:j + 1] - c_im * pi_r[j:j + 1] for j in range(c)], axis=0)
    vj_im = jnp.concatenate([c_re * pi_r[j:j + 1] + c_im * pr_r[j:j + 1] for j in range(c)], axis=0)
    kt = (jnp.dot(vj_re, bb_re, precision=HIGHEST, preferred_element_type=F32)
          - jnp.dot(vj_im, bb_im, precision=HIGHEST, preferred_element_type=F32))
    strip = jnp.concatenate([kt[j * LANES:(j + 1) * LANES].T for j in range(c)], axis=1)
    for s in range(c):
        blk = strip if s == 0 else jnp.concatenate(
            [jnp.zeros((LANES, s * LANES), F32), strip[:, :(c - s) * LANES]], axis=1)
        m_ref[0, s * LANES:(s + 1) * LANES, :] = blk.astype(BF16)

    are_ref[...] = pr_r[c:c + 1]
    aim_ref[...] = pi_r[c:c + 1]


def _s5_prep(lam_re, lam_im, log_dt, b_re, b_im, c_re, c_im):
    g, p = lam_re.shape
    gpb, hh = S5_GPB, SSM_GROUP
    blk3 = lambda d1, d2: pl.BlockSpec((gpb, d1, d2), lambda j: (j, 0, 0))
    out3 = lambda d1, d2: pl.BlockSpec((1, d1, d2), lambda j: (j, 0, 0))
    arow = pl.BlockSpec((1, gpb * p), lambda j: (0, j))
    return pl.pallas_call(
        _s5_prep_kernel,
        out_shape=(jax.ShapeDtypeStruct((S5_BLOCKS, S5_K, S5_K), BF16),
                   jax.ShapeDtypeStruct((S5_BLOCKS, S5_K, 2 * S5_HALF), BF16),
                   jax.ShapeDtypeStruct((S5_BLOCKS, 2 * S5_HALF, S5_K), BF16),
                   jax.ShapeDtypeStruct((1, g * p), F32), jax.ShapeDtypeStruct((1, g * p), F32)),
        grid=(S5_BLOCKS,),
        in_specs=[blk3(1, p), blk3(1, p), blk3(p, 1), blk3(p, 1), blk3(1, 1), blk3(1, 1),
                  blk3(hh, p), blk3(hh, p), blk3(p, hh), blk3(p, hh),
                  blk3(hh, p), blk3(hh, p), blk3(p, hh), blk3(p, hh)],
        out_specs=(out3(S5_K, S5_K), out3(S5_K, 2 * S5_HALF), out3(2 * S5_HALF, S5_K), arow, arow),
        compiler_params=_params(("parallel",)),
        name="s5_chunk_operators",
    )(lam_re.reshape(g, 1, p), lam_im.reshape(g, 1, p), lam_re.reshape(g, p, 1), lam_im.reshape(g, p, 1),
      log_dt.reshape(g, 1, 1), log_dt.reshape(g, 1, 1),
      jnp.swapaxes(b_re, 1, 2), jnp.swapaxes(b_im, 1, 2), b_re, b_im,
      c_re, c_im, jnp.swapaxes(c_re, 1, 2), jnp.swapaxes(c_im, 1, 2))


S5_ROWS = 512


def _s5_gather_chunks(u_ref, rows):
    return jnp.concatenate([u_ref[pl.ds(s, rows, stride=S5_CHUNK), :] for s in range(S5_CHUNK)], axis=1)


def _s5_local_kernel(u_ref, w_ref, x_ref, dre_ref, dim_ref):
    rows = x_ref.shape[0]
    x = _s5_gather_chunks(u_ref, rows).astype(BF16)
    x_ref[...] = x
    d = jnp.dot(x, w_ref[0], preferred_element_type=F32)
    dre_ref[...] = d[:, :S5_HALF]
    dim_ref[...] = d[:, S5_HALF:]


def _s5_local(u2, w_blk, rows):
    t = u2.shape[0]
    nc = t // S5_CHUNK
    return pl.pallas_call(
        _s5_local_kernel,
        out_shape=(jax.ShapeDtypeStruct((nc, S5_BLOCKS * S5_K), BF16),
                   jax.ShapeDtypeStruct((nc, S5_BLOCKS * S5_HALF), F32),
                   jax.ShapeDtypeStruct((nc, S5_BLOCKS * S5_HALF), F32)),
        grid=(S5_BLOCKS, nc // rows),
        in_specs=[pl.BlockSpec((rows * S5_CHUNK, LANES), lambda j, i: (i, j)),
                  pl.BlockSpec((1, S5_K, 2 * S5_HALF), lambda j, i: (j, 0, 0))],
        out_specs=(pl.BlockSpec((rows, S5_K), lambda j, i: (i, j)),
                   pl.BlockSpec((rows, S5_HALF), lambda j, i: (i, j)),
                   pl.BlockSpec((rows, S5_HALF), lambda j, i: (i, j))),
        compiler_params=_params(("parallel", "parallel")),
        name="s5_chunk_state_increment",
    )(u2, w_blk)


def _s5_scan_kernel(dre_ref, dim_ref, h0re_ref, h0im_ref, ar_ref, ai_ref, hre_ref, him_ref, fre_ref, fim_ref,
                    cre_ref, cim_ref):
    @pl.when(pl.program_id(2) == 0)
    def _():
        cre_ref[...] = h0re_ref[0]
        cim_ref[...] = h0im_ref[0]

    ar = ar_ref[...]
    ai = ai_ref[...]

    def body(r, h):
        hr, hi = h
        hre_ref[0, pl.ds(r, 1), :] = hr
        him_ref[0, pl.ds(r, 1), :] = hi
        dr = dre_ref[0, pl.ds(r, 1), :]
        di = dim_ref[0, pl.ds(r, 1), :]
        return ar * hr - ai * hi + dr, ar * hi + ai * hr + di

    hr, hi = lax.fori_loop(0, dre_ref.shape[1], body, (cre_ref[...], cim_ref[...]))
    cre_ref[...] = hr
    cim_ref[...] = hi
    fre_ref[0] = hr
    fim_ref[0] = hi


def _s5_scan(d_re, d_im, h0_re, h0_im, a_re, a_im, tc, tw):
    b, ncb, w = d_re.shape
    dspec = pl.BlockSpec((1, tc, tw), lambda i, k, c: (i, c, k))
    hspec = pl.BlockSpec((1, 1, tw), lambda i, k, c: (i, 0, k))
    aspec = pl.BlockSpec((1, tw), lambda i, k, c: (0, k))
    big = jax.ShapeDtypeStruct((b, ncb, w), F32)
    small = jax.ShapeDtypeStruct((b, 1, w), F32)
    return pl.pallas_call(
        _s5_scan_kernel,
        out_shape=(big, big, small, small),
        grid=(b, w // tw, ncb // tc),
        in_specs=[dspec, dspec, hspec, hspec, aspec, aspec],
        out_specs=(dspec, dspec, hspec, hspec),
        scratch_shapes=[pltpu.VMEM((1, tw), F32), pltpu.VMEM((1, tw), F32)],
        compiler_params=_params(("parallel", "parallel", "arbitrary")),
        name="s5_chunk_state_scan",
    )(d_re, d_im, h0_re, h0_im, a_re, a_im)


def _gelu_tanh(x):
    return x * (0.5 * (1.0 + jnp.tanh(math.sqrt(2.0 / math.pi) * (x + 0.044715 * (x * x * x)))))


def _s5_out_kernel(x_ref, hre_ref, him_ref, m_ref, v_ref, u_ref, d_ref, z_ref, z_scr):
    rows = x_ref.shape[0]
    hre = hre_ref[...].astype(BF16)
    him = him_ref[...].astype(BF16)
    skip = d_ref[...]
    pair = 2 * LANES
    for q in range(S5_K // pair):
        cols = slice(q * pair, (q + 1) * pair)
        k = (q + 1) * pair
        y = (jnp.dot(x_ref[:, :k], m_ref[0, :k, cols], preferred_element_type=F32)
             + jnp.dot(hre, v_ref[0, :S5_HALF, cols], preferred_element_type=F32)
             + jnp.dot(him, v_ref[0, S5_HALF:, cols], preferred_element_type=F32))
        for e in range(2):
            tok = pl.ds(2 * q + e, rows, stride=S5_CHUNK)
            z_scr[tok, :] = _gelu_tanh(y[:, e * LANES:(e + 1) * LANES] + skip * u_ref[tok, :])
    z_ref[...] = z_scr[...].astype(BF16)


def _s5_out(x, h_re, h_im, m_blk, v_blk, u2, ssm_d, rows):
    nc = x.shape[0]
    t = nc * S5_CHUNK
    return pl.pallas_call(
        _s5_out_kernel,
        out_shape=jax.ShapeDtypeStruct((t, SSM_WIDTH), BF16),
        grid=(S5_BLOCKS, nc // rows),
        in_specs=[pl.BlockSpec((rows, S5_K), lambda j, i: (i, j)),
                  pl.BlockSpec((rows, S5_HALF), lambda j, i: (i, j)),
                  pl.BlockSpec((rows, S5_HALF), lambda j, i: (i, j)),
                  pl.BlockSpec((1, S5_K, S5_K), lambda j, i: (j, 0, 0), pipeline_mode=pl.Buffered(1)),
                  pl.BlockSpec((1, 2 * S5_HALF, S5_K), lambda j, i: (j, 0, 0), pipeline_mode=pl.Buffered(1)),
                  pl.BlockSpec((rows * S5_CHUNK, LANES), lambda j, i: (i, j)),
                  pl.BlockSpec((1, LANES), lambda j, i: (0, j))],
        out_specs=pl.BlockSpec((rows * S5_CHUNK, LANES), lambda j, i: (i, j)),
        scratch_shapes=[pltpu.VMEM((rows * S5_CHUNK, LANES), F32)],
        compiler_params=_params(("parallel", "parallel")),
        name="s5_chunk_output",
    )(x, h_re, h_im, m_blk, v_blk, u2, ssm_d.reshape(1, SSM_WIDTH))


def _state_to_blocks(h):
    return h.astype(F32).reshape(h.shape[0], 1, N_GROUPS * SSM_STATE)


def _s5(u, h0_re, h0_im, blk_ops, ssm_d):
    m_blk, w_blk, v_blk, a_re, a_im = blk_ops
    b, l, _ = u.shape
    ncb = l // S5_CHUNK
    w = N_GROUPS * SSM_STATE
    u2 = u.reshape(b * l, SSM_WIDTH)
    rows = min(b * ncb, S5_ROWS)
    x, d_re, d_im = _s5_local(u2, w_blk, rows)
    if h0_re is None:
        h0_re = h0_im = jnp.zeros((b, N_GROUPS, SSM_STATE), F32)
    hs_re, hs_im, hf_re, hf_im = _s5_scan(d_re.reshape(b, ncb, w), d_im.reshape(b, ncb, w),
                                          _state_to_blocks(h0_re), _state_to_blocks(h0_im), a_re, a_im,
                                          tc=min(ncb, 128), tw=w if ncb <= 8 else 2048)
    z = _s5_out(x, hs_re.reshape(b * ncb, w), hs_im.reshape(b * ncb, w), m_blk, v_blk, u2, ssm_d, rows)
    return (z.reshape(b, l, SSM_WIDTH), hf_re.reshape(b, N_GROUPS, SSM_STATE), hf_im.reshape(b, N_GROUPS, SSM_STATE))


def _mix_kernel(x_ref, z_ref, o_ref, gate_ref, wg_ref, bg_ref, wo_ref, out_ref):
    nb, tl, d = x_ref.shape
    rows = nb * tl
    gl = jnp.dot(z_ref[...].reshape(rows, SSM_WIDTH), wg_ref[...], preferred_element_type=F32) + bg_ref[...]
    yg = (gl[:, :SSM_WIDTH] * _sigmoid(gl[:, SSM_WIDTH:])).astype(BF16)
    mix = (jnp.dot(o_ref[...].reshape(rows, ATTN_WIDTH), wo_ref[:ATTN_WIDTH, :], preferred_element_type=F32)
           + jnp.dot(yg, wo_ref[ATTN_WIDTH:, :], preferred_element_type=F32))
    out_ref[...] = x_ref[...] + gate_ref[...] * mix.reshape(nb, tl, d)


def _mix(x, z_ssm, o_attn, gate, w_glu_b, b_glu, w_out_b, nb, tl):
    b, l, d = x.shape
    xspec = lambda w: pl.BlockSpec((nb, tl, w), lambda i, j: (i, j, 0))
    const = lambda shape: pl.BlockSpec(shape, lambda i, j: (0, 0), pipeline_mode=pl.Buffered(1))
    return pl.pallas_call(
        _mix_kernel,
        out_shape=jax.ShapeDtypeStruct((b, l, d), F32),
        grid=(b // nb, l // tl),
        in_specs=[xspec(d), xspec(SSM_WIDTH), xspec(ATTN_WIDTH),
                  pl.BlockSpec((nb, 1, d), lambda i, j: (i, 0, 0)),
                  const((SSM_WIDTH, 2 * SSM_WIDTH)), const((1, 2 * SSM_WIDTH)), const((d, d))],
        out_specs=xspec(d),
        compiler_params=_params(("parallel", "parallel")),
        name="glu_outproj_residual",
    )(x, z_ssm, o_attn, gate, w_glu_b, b_glu.reshape(1, -1), w_out_b)


FFN_ROWS = 512
FFN_TF = 1024

def _ffn_kernel(x_ref, sc_ref, sh_ref, gate_ref, g_ref, gf_ref, w1_ref, w2_ref, y_ref, h_ref, acc_ref):
    nb, tl, d = x_ref.shape
    j = pl.program_id(2)

    @pl.when(j == 0)
    def _():
        h = _modulated_norm(x_ref[...], g_ref[...], sc_ref[...], sh_ref[...])
        h_ref[...] = h.reshape(nb * tl, d).astype(BF16)
        acc_ref[...] = jnp.zeros_like(acc_ref)

    hid = jnp.dot(h_ref[...], w1_ref[...], preferred_element_type=F32)
    hid = jnp.square(jnp.maximum(hid, 0.0)).astype(BF16)
    acc_ref[...] += jnp.dot(hid, w2_ref[...], preferred_element_type=F32)

    @pl.when(j == pl.num_programs(2) - 1)
    def _():
        x = x_ref[...] + gate_ref[...] * acc_ref[...].reshape(nb, tl, d)
        ms = jnp.mean(x * x, axis=-1, keepdims=True)
        y_ref[...] = x * lax.rsqrt(ms + EPS) * gf_ref[...]


def _ffn(x, scale, shift, gate, g_ffn, g_final, w1_b, w2_b, nb, tl, tf):
    b, l, d = x.shape
    xspec = pl.BlockSpec((nb, tl, d), lambda i, j, k: (i, j, 0))
    mspec = pl.BlockSpec((nb, 1, d), lambda i, j, k: (i, 0, 0))
    gspec = pl.BlockSpec((1, d), lambda i, j, k: (0, 0))
    return pl.pallas_call(
        _ffn_kernel,
        out_shape=jax.ShapeDtypeStruct((b, l, d), F32),
        grid=(b // nb, l // tl, D_FF // tf),
        in_specs=[xspec, mspec, mspec, mspec, gspec, gspec,
                  pl.BlockSpec((d, tf), lambda i, j, k: (0, k)),
                  pl.BlockSpec((tf, d), lambda i, j, k: (k, 0))],
        out_specs=xspec,
        scratch_shapes=[pltpu.VMEM((nb * tl, d), BF16), pltpu.VMEM((nb * tl, d), F32)],
        compiler_params=_params(("parallel", "parallel", "arbitrary")),
        name="relu2_mlp_final_norm",
    )(x, scale, shift, gate, g_ffn.reshape(1, d), g_final.reshape(1, d), w1_b, w2_b)


def _layer(x, mods, tables, tables_t, past_kt, past_v2, h0_re, h0_im, wts, nb, tl):
    shift1, scale1, gate1, shift2, scale2, gate2 = mods
    b, l, _ = x.shape
    first_chunk = past_kt is None
    w_k = wts["w_k_t"] if first_chunk else wts["w_k"]
    q, k, v, u, kb, vb = _inproj(x, scale1, shift1, wts["g_mix"], wts["w_qvu"], w_k, tables, tables_t, nb, tl)
    if first_chunk:
        o = _attn_prompt(q, kb, vb, wts["lam_vecs"], wts["g_subln"])
        k6 = k.reshape(b, N_HEADS, 2, HEAD_DIM, l).transpose(0, 4, 1, 2, 3)[None]
    else:
        o = _attn_sample(q, past_kt, past_v2, kb, vb, wts["lam_vecs"], wts["g_subln"])
        k6 = k.reshape(1, b, l, N_HEADS, 2, HEAD_DIM)
    z_ssm, h_re, h_im = _s5(u, h0_re, h0_im, wts["s5_ops"], wts["ssm_d"])
    x1 = _mix(x, z_ssm, o, gate1, wts["w_glu"], wts["b_glu"], wts["w_out"], nb, tl)
    tl_ffn = min(l, FFN_ROWS) if nb == 1 else tl
    y = _ffn(x1, scale2, shift2, gate2, wts["g_ffn"], wts["g_final"], wts["w_ff1"], wts["w_ff2"],
             nb, tl_ffn, tf=FFN_TF)
    v5 = v.reshape(1, b, l, N_HEADS, V_DIM)
    return y, k6, v5, h_re[None], h_im[None]


def kernel(x_prompt, x_sample, c_prompt, c_sample, cache_k, cache_v, state_ssm_re, state_ssm_im, w_ada, b_ada, g_mix, w_in, lam_q1, lam_k1, lam_q2, lam_k2, g_subln, ssm_lam_re, ssm_lam_im, ssm_log_dt, ssm_b_re, ssm_b_im, ssm_c_re, ssm_c_im, ssm_d, w_glu, b_glu, w_out, g_ffn, w_ff1, w_ff2, g_final):
    bp, lp, d = x_prompt.shape
    bs, ls, _ = x_sample.shape
    past = cache_k.shape[2]

    m = _ada(jnp.concatenate([c_prompt, c_sample], axis=0), w_ada[0], b_ada[0])
    mods = [m[:, i * d:(i + 1) * d].reshape(bp + bs, 1, d) for i in range(6)]
    mods_p = [t[:bp] for t in mods]
    mods_s = [t[bp:] for t in mods]

    w_in_b = w_in[0].astype(BF16)
    w_k = w_in_b[:, ATTN_WIDTH:2 * ATTN_WIDTH]
    wts = {
        "g_mix": g_mix[0],
        "w_qvu": jnp.concatenate([w_in_b[:, :ATTN_WIDTH], w_in_b[:, 2 * ATTN_WIDTH:]], axis=1),
        "w_k": w_k, "w_k_t": w_k.T,
        "lam_vecs": tuple(t[0].reshape(1, HEAD_DIM) for t in (lam_q1, lam_k1, lam_q2, lam_k2)),
        "g_subln": g_subln[0],
        "s5_ops": _s5_prep(ssm_lam_re[0], ssm_lam_im[0], ssm_log_dt[0], ssm_b_re[0], ssm_b_im[0],
                           ssm_c_re[0], ssm_c_im[0]),
        "ssm_d": ssm_d[0], "w_glu": w_glu[0].astype(BF16), "b_glu": b_glu[0],
        "w_out": w_out[0].astype(BF16), "g_ffn": g_ffn[0], "g_final": g_final,
        "w_ff1": w_ff1[0].astype(BF16), "w_ff2": w_ff2[0].astype(BF16),
    }

    tl_p = min(lp, 512)
    yp, kp, vp, srp, sip = _layer(x_prompt, mods_p, _rope_tables(lp, 0), _rope_tables_t(lp, 0),
                                  None, None, None, None, wts, nb=1, tl=tl_p)
    nb_s = max(1, min(bs, 512 // ls))
    ckt = cache_k[0].reshape(bs, past, ATTN_WIDTH).transpose(0, 2, 1)
    cv2 = cache_v[0].reshape(bs, past * N_HEADS, V_DIM)
    ys, ks, vs, srs, sis = _layer(x_sample, mods_s, _rope_tables(ls, past), None, ckt, cv2,
                                  state_ssm_re[0], state_ssm_im[0], wts, nb=nb_s, tl=ls)
    return (yp, ys, kp, vp, srp, sip, ks, vs, srs, sis)
```

```python
import functools
import math

import jax
import jax.numpy as jnp
from jax import lax
from jax.experimental import pallas as pl
from jax.experimental.pallas import tpu as pltpu

F32 = jnp.float32
BF16 = jnp.bfloat16

D_MODEL = 2048
ATTN_WIDTH = D_MODEL // 2
SSM_WIDTH = D_MODEL - ATTN_WIDTH
N_HEADS = 8
HEAD_DIM = ATTN_WIDTH // (2 * N_HEADS)
V_DIM = 2 * HEAD_DIM
ROT_DIM = HEAD_DIM // 4
ROPE_THETA = 500000.0
CHUNK = 64
SSM_GROUP = 16
N_GROUPS = SSM_WIDTH // SSM_GROUP
SSM_STATE = 64
IN_WIDTH = 3 * ATTN_WIDTH + SSM_WIDTH
D_FF = 4 * D_MODEL
EPS = 1e-6
NEG_INF = -1e30
LAM_INIT = 0.8 - 0.6 * math.exp(-0.3 * 0)

LANES = 128
S5_CHUNK = 16
S5_BLOCKS = SSM_WIDTH // LANES
S5_GPB = LANES // SSM_GROUP
S5_K = S5_CHUNK * LANES
S5_HALF = S5_GPB * SSM_STATE
VMEM_LIMIT = 56 * 1024 * 1024

HIGHEST = lax.Precision.HIGHEST


def _params(sem, vmem=VMEM_LIMIT):
    return pltpu.CompilerParams(dimension_semantics=sem, vmem_limit_bytes=vmem)


def _sigmoid(x):
    return 1.0 / (1.0 + jnp.exp(-x))


def _ada_kernel(c_ref, w_ref, b_ref, o_ref):
    c = c_ref[...]
    s = (c * _sigmoid(c)).astype(BF16)
    o_ref[...] = jnp.dot(s, w_ref[...].astype(BF16), preferred_element_type=F32) + b_ref[...]


def _ada(c_all, w_ada, b_ada):
    nb, d = c_all.shape
    n = w_ada.shape[1]
    tn = 1024
    return pl.pallas_call(
        _ada_kernel,
        out_shape=jax.ShapeDtypeStruct((nb, n), F32),
        grid=(n // tn,),
        in_specs=[pl.BlockSpec((nb, d), lambda j: (0, 0)),
                  pl.BlockSpec((d, tn), lambda j: (0, j)),
                  pl.BlockSpec((1, tn), lambda j: (0, j))],
        out_specs=pl.BlockSpec((nb, tn), lambda j: (0, j)),
        compiler_params=_params(("parallel",)),
        name="ada_modulation",
    )(c_all, w_ada, b_ada.reshape(1, n))


def _rope_angles(pos, freq_index):
    inv = jnp.exp(freq_index * (-2.0 * math.log(ROPE_THETA) / ROT_DIM))
    return pos * inv


def _rope_kernel(c_ref, sa_ref, sb_ref, *, offset):
    tl = c_ref.shape[0]
    row = lax.broadcasted_iota(jnp.int32, (tl, LANES), 0)
    lane = lax.broadcasted_iota(jnp.int32, (tl, LANES), 1)
    pos = (offset + pl.program_id(0) * tl + row).astype(F32)
    d = lane % HEAD_DIM
    ang = _rope_angles(pos, (d % (ROT_DIM // 2)).astype(F32))
    c = jnp.cos(ang)
    s = jnp.sin(ang)
    in_rot = d < ROT_DIM
    first = d < ROT_DIM // 2
    c_ref[...] = jnp.where(in_rot, c, 1.0)
    sa_ref[...] = jnp.where(in_rot, jnp.where(first, 0.0, s), 0.0)
    sb_ref[...] = jnp.where(first, -s, 0.0)


def _rope_tables(length, offset):
    tl = min(length, 1024)
    shp = jax.ShapeDtypeStruct((length, LANES), F32)
    spec = pl.BlockSpec((tl, LANES), lambda i: (i, 0))
    return pl.pallas_call(
        functools.partial(_rope_kernel, offset=offset),
        out_shape=(shp, shp, shp),
        grid=(length // tl,),
        out_specs=(spec, spec, spec),
        compiler_params=_params(("parallel",)),
        name="rope_tables",
    )()


def _rope_t_kernel(c_ref, s_ref, *, offset):
    n = c_ref.shape[1]
    fi = lax.broadcasted_iota(jnp.int32, (ROT_DIM // 2, n), 0).astype(F32)
    pos = (offset + pl.program_id(0) * n + lax.broadcasted_iota(jnp.int32, (ROT_DIM // 2, n), 1)).astype(F32)
    ang = _rope_angles(pos, fi)
    c_ref[...] = jnp.cos(ang)
    s_ref[...] = jnp.sin(ang)


def _rope_tables_t(length, offset):
    tn = min(length, 2048)
    shp = jax.ShapeDtypeStruct((ROT_DIM // 2, length), F32)
    spec = pl.BlockSpec((ROT_DIM // 2, tn), lambda i: (0, i))
    return pl.pallas_call(
        functools.partial(_rope_t_kernel, offset=offset),
        out_shape=(shp, shp),
        grid=(length // tn,),
        out_specs=(spec, spec),
        compiler_params=_params(("parallel",)),
        name="rope_tables_transposed",
    )()


Q_SCALE = HEAD_DIM ** -0.5 * math.log2(math.e)


def _modulated_norm(x, g, scale, shift):
    ms = jnp.mean(x * x, axis=-1, keepdims=True)
    y = x * lax.rsqrt(ms + EPS) * g
    return y * (1.0 + scale) + shift


def _inproj_kernel(*refs, k_t):
    if k_t:
        (x_ref, sc_ref, sh_ref, g_ref, w_ref, wk_ref, c_ref, sa_ref, sb_ref, ct_ref, st_ref,
         q_ref, k_ref, v_ref, u_ref, kb_ref, vb_ref) = refs
    else:
        (x_ref, sc_ref, sh_ref, g_ref, w_ref, wk_ref, c_ref, sa_ref, sb_ref,
         q_ref, k_ref, v_ref, u_ref, kb_ref, vb_ref) = refs
    nb, tl, d = x_ref.shape
    rows = nb * tl
    h = _modulated_norm(x_ref[...], g_ref[...], sc_ref[...], sh_ref[...])
    hb = h.reshape(rows, d).astype(BF16)
    cos = c_ref[...]
    sin_a = sa_ref[...]
    sin_b = sb_ref[...]

    def rope(t):
        ra = pltpu.roll(t, ROT_DIM // 2, 1).reshape(nb, tl, LANES)
        rb = pltpu.roll(t, LANES - ROT_DIM // 2, 1).reshape(nb, tl, LANES)
        return t.reshape(nb, tl, LANES) * cos + ra * sin_a + rb * sin_b

    q = jnp.dot(hb, w_ref[:, 0:ATTN_WIDTH], preferred_element_type=F32)
    for hd in range(N_HEADS):
        sl = slice(hd * LANES, (hd + 1) * LANES)
        q_ref[:, :, sl] = (rope(q[:, sl]) * Q_SCALE).astype(BF16)
    if k_t:
        kt = _nt_dot(wk_ref[...], hb)
        cos_t = ct_ref[...]
        sin_t = st_ref[...]
        half = ROT_DIM // 2
        for comp in range(2 * N_HEADS):
            b0 = comp * HEAD_DIM
            t1 = kt[b0:b0 + half]
            t2 = kt[b0 + half:b0 + ROT_DIM]
            blk = jnp.concatenate([t1 * cos_t - t2 * sin_t, t2 * cos_t + t1 * sin_t,
                                   kt[b0 + ROT_DIM:b0 + HEAD_DIM]], axis=0)
            k_ref[0, b0:b0 + HEAD_DIM, :] = blk
            kb_ref[0, b0:b0 + HEAD_DIM, :] = blk.astype(BF16)
    else:
        k = jnp.dot(hb, wk_ref[...], preferred_element_type=F32)
        for hd in range(N_HEADS):
            sl = slice(hd * LANES, (hd + 1) * LANES)
            kr = rope(k[:, sl])
            k_ref[:, :, sl] = kr
            kb_ref[:, :, sl] = kr.astype(BF16)
    v = jnp.dot(hb, w_ref[:, ATTN_WIDTH:2 * ATTN_WIDTH], preferred_element_type=F32)
    v = v.reshape(nb, tl, ATTN_WIDTH)
    v_ref[...] = v
    vb_ref[...] = v.astype(BF16)
    u = jnp.dot(hb, w_ref[:, 2 * ATTN_WIDTH:], preferred_element_type=F32)
    u_ref[...] = u.reshape(nb, tl, SSM_WIDTH)


def _inproj(x, scale, shift, g_mix, w_qvu, w_k, tables, tables_t, nb, tl):
    b, l, d = x.shape
    k_t = tables_t is not None
    assert not k_t or nb == 1
    grid = (b // nb, l // tl)
    xspec = lambda w: pl.BlockSpec((nb, tl, w), lambda i, j: (i, j, 0))
    mspec = pl.BlockSpec((nb, 1, d), lambda i, j: (i, 0, 0))
    tspec = pl.BlockSpec((tl, LANES), lambda i, j: (j, 0))
    const = lambda a: pl.BlockSpec(a.shape, lambda i, j: (0, 0), pipeline_mode=pl.Buffered(1))
    out = lambda dt: jax.ShapeDtypeStruct((b, l, ATTN_WIDTH), dt)
    in_specs = [xspec(d), mspec, mspec, pl.BlockSpec((1, d), lambda i, j: (0, 0)),
                const(w_qvu), const(w_k), tspec, tspec, tspec]
    args = [x, scale, shift, g_mix.reshape(1, d), w_qvu, w_k, *tables]
    kspec, kshape = xspec(ATTN_WIDTH), out
    if k_t:
        in_specs += [pl.BlockSpec((ROT_DIM // 2, tl), lambda i, j: (0, j))] * 2
        args += list(tables_t)
        kspec = pl.BlockSpec((1, ATTN_WIDTH, tl), lambda i, j: (i, 0, j))
        kshape = lambda dt: jax.ShapeDtypeStruct((b, ATTN_WIDTH, l), dt)
    return pl.pallas_call(
        functools.partial(_inproj_kernel, k_t=k_t),
        out_shape=(out(BF16), kshape(F32), out(F32), out(F32), kshape(BF16), out(BF16)),
        grid=grid,
        in_specs=in_specs,
        out_specs=(xspec(ATTN_WIDTH), kspec, xspec(ATTN_WIDTH), xspec(ATTN_WIDTH), kspec, xspec(ATTN_WIDTH)),
        compiler_params=_params(("parallel", "parallel")),
        name="norm_inproj_rope",
    )(*args)


ATTN_TQ = 512
ATTN_TK = 512
ATTN_NSUB = 4


def _stack_components(q):
    lane = lax.broadcasted_iota(jnp.int32, q.shape, 1)
    zero = jnp.zeros_like(q)
    return jnp.concatenate([jnp.where(lane < HEAD_DIM, q, zero),
                            jnp.where(lane >= HEAD_DIM, q, zero)], axis=0)


def _diff_epilogue(num, den, tq, lq1, lk1, lq2, lk2, g_subln):
    lam = (jnp.exp(jnp.sum(lq1 * lk1, axis=-1, keepdims=True))
           - jnp.exp(jnp.sum(lq2 * lk2, axis=-1, keepdims=True)) + LAM_INIT)
    on = num / den
    o = on[:tq] - lam * on[tq:]
    ms = jnp.mean(o * o, axis=-1, keepdims=True)
    return o * lax.rsqrt(ms + EPS) * g_subln * (1.0 - LAM_INIT)


def _nt_dot(a, b):
    return lax.dot_general(a, b, (((1,), (1,)), ((), ())), preferred_element_type=F32)


def _attn_prompt_kernel(lq1_ref, lk1_ref, lq2_ref, lk2_ref, gs_ref, q_ref, kt_ref, v_ref, o_ref,
                        qs_scr, s_scr, p_scr, m_scr, acc_scr, *, tq, tk, nsub):
    i = pl.program_id(2)
    rows = 2 * tq
    qs_scr[...] = _stack_components(q_ref[0])
    span = nsub * tk
    nfull = (i * tq) // span
    ones = jnp.ones((tk, LANES), BF16)
    m_scr[...] = jnp.full((rows, LANES), NEG_INF, F32)
    acc_scr[...] = jnp.zeros((rows, 2 * LANES), F32)

    def scores(start, u, masked):
        s = jnp.dot(qs_scr[...], kt_ref[0, :, pl.ds(start, tk)], preferred_element_type=F32)
        if masked:
            r = lax.broadcasted_iota(jnp.int32, s.shape, 0)
            c = lax.broadcasted_iota(jnp.int32, s.shape, 1)
            qc = (i * tq + jnp.where(r >= tq, r - tq, r)) // CHUNK
            s = jnp.where((start + c) // CHUNK <= qc, s, NEG_INF)
        s_scr[u] = s
        return jnp.broadcast_to(jnp.max(s, axis=-1, keepdims=True), (rows, LANES))

    def accumulate(start, u, block_max):
        m_old = m_scr[...]
        m_new = jnp.maximum(m_old, block_max)
        alpha = jnp.exp2(m_old - m_new)
        m_scr[...] = m_new
        p = jnp.exp2(s_scr[u] - jnp.concatenate([m_new] * (tk // LANES), axis=1))
        p_scr[u] = p.astype(BF16)
        v1 = jnp.concatenate([v_ref[0, pl.ds(start, tk), :], ones], axis=1)
        pv = jnp.dot(p_scr[u], v1, preferred_element_type=F32)
        acc_scr[...] = jnp.concatenate([alpha, alpha], axis=1) * acc_scr[...] + pv

    def body(t, _):
        base = pl.multiple_of(t * span, span)
        maxes = [scores(base + u * tk, u, False) for u in range(nsub)]
        for u in range(nsub):
            accumulate(base + u * tk, u, maxes[u])
        return 0
    lax.fori_loop(0, nfull, body, 0)

    left = i - nfull * nsub
    for n_left in range(nsub):
        @pl.when(left == n_left)
        def _(n_left=n_left):
            base = pl.multiple_of(nfull * span, span)
            maxes = [scores(base + u * tk, u, u == n_left) for u in range(n_left + 1)]
            for u in range(n_left + 1):
                accumulate(base + u * tk, u, maxes[u])

    acc = acc_scr[...]
    o = _diff_epilogue(acc[:, :LANES], acc[:, LANES:], tq, lq1_ref[...], lk1_ref[...], lq2_ref[...],
                       lk2_ref[...], gs_ref[...])
    o_ref[0] = o.astype(BF16)


def _attn_prompt(q, ktb, vb, lam_vecs, g_subln):
    b, l, _ = q.shape
    tq = min(l, ATTN_TQ)
    tk = min(l, ATTN_TK)
    nsub = ATTN_NSUB
    assert l % tq == 0 and tq == tk and tq % CHUNK == 0
    vec = pl.BlockSpec((1, HEAD_DIM), lambda bi, h, i: (0, 0))
    return pl.pallas_call(
        functools.partial(_attn_prompt_kernel, tq=tq, tk=tk, nsub=nsub),
        out_shape=jax.ShapeDtypeStruct((b, l, ATTN_WIDTH), BF16),
        grid=(b, N_HEADS, l // tq),
        in_specs=[vec, vec, vec, vec,
                  pl.BlockSpec((1, V_DIM), lambda bi, h, i: (0, 0)),
                  pl.BlockSpec((1, tq, LANES), lambda bi, h, i: (bi, i, h)),
                  pl.BlockSpec((1, LANES, l), lambda bi, h, i: (bi, h, 0)),
                  pl.BlockSpec((1, l, LANES), lambda bi, h, i: (bi, 0, h))],
        out_specs=pl.BlockSpec((1, tq, LANES), lambda bi, h, i: (bi, i, h)),
        scratch_shapes=[pltpu.VMEM((2 * tq, LANES), BF16), pltpu.VMEM((nsub, 2 * tq, tk), F32),
                        pltpu.VMEM((nsub, 2 * tq, tk), BF16),
                        pltpu.VMEM((2 * tq, LANES), F32), pltpu.VMEM((2 * tq, 2 * LANES), F32)],
        compiler_params=_params(("parallel", "parallel", "arbitrary")),
        name="prompt_diff_attention",
    )(*lam_vecs, g_subln.reshape(1, V_DIM), q, ktb, vb)


def _attn_sample_kernel(lq1_ref, lk1_ref, lq2_ref, lk2_ref, gs_ref, q_ref, ckt_ref, cv_ref,
                        kn_ref, vn_ref, o_ref):
    tq = q_ref.shape[1]
    past = ckt_ref.shape[2]
    for hd in range(N_HEADS):
        cols = slice(hd * LANES, (hd + 1) * LANES)
        qs = _stack_components(q_ref[0, :, cols])
        s_p = jnp.dot(qs, ckt_ref[0, cols, :].astype(BF16), preferred_element_type=F32)
        s_n = _nt_dot(qs, kn_ref[0, :, cols])
        m = jnp.maximum(jnp.max(s_p, axis=-1, keepdims=True), jnp.max(s_n, axis=-1, keepdims=True))
        p_p = jnp.exp2(s_p - m)
        p_n = jnp.exp2(s_n - m)
        l = jnp.sum(p_p, axis=-1, keepdims=True) + jnp.sum(p_n, axis=-1, keepdims=True)
        vp = cv_ref[0, pl.ds(hd, past, stride=N_HEADS), :].astype(BF16)
        acc = (jnp.dot(p_p.astype(BF16), vp, preferred_element_type=F32)
               + jnp.dot(p_n.astype(BF16), vn_ref[0, :, cols], preferred_element_type=F32))
        o = _diff_epilogue(acc, l, tq, lq1_ref[...], lk1_ref[...], lq2_ref[...], lk2_ref[...], gs_ref[...])
        o_ref[0, :, cols] = o.astype(BF16)


def _attn_sample(q, cache_kt, cache_v2, kb, vb, lam_vecs, g_subln):
    b, l, _ = q.shape
    past = cache_kt.shape[2]
    vec = pl.BlockSpec((1, HEAD_DIM), lambda bi: (0, 0))
    new = pl.BlockSpec((1, l, ATTN_WIDTH), lambda bi: (bi, 0, 0))
    return pl.pallas_call(
        _attn_sample_kernel,
        out_shape=jax.ShapeDtypeStruct((b, l, ATTN_WIDTH), BF16),
        grid=(b,),
        in_specs=[vec, vec, vec, vec, pl.BlockSpec((1, V_DIM), lambda bi: (0, 0)),
                  new,
                  pl.BlockSpec((1, ATTN_WIDTH, past), lambda bi: (bi, 0, 0)),
                  pl.BlockSpec((1, past * N_HEADS, LANES), lambda bi: (bi, 0, 0)),
                  new, new],
        out_specs=new,
        compiler_params=_params(("parallel",)),
        name="sample_diff_attention",
    )(*lam_vecs, g_subln.reshape(1, V_DIM), q, cache_kt, cache_v2, kb, vb)


def _s5_discretize(lr, li, dt):
    mag = jnp.exp(lr * dt)
    ar = mag * jnp.cos(li * dt)
    ai = mag * jnp.sin(li * dt)
    den = lr * lr + li * li
    f_re = ((ar - 1.0) * lr + ai * li) / den
    f_im = (ai * lr - (ar - 1.0) * li) / den
    return f_re, f_im


def _s5_power(lr, li, dt, e):
    mag = jnp.exp(lr * dt * e)
    ang = li * dt * e
    return mag * jnp.cos(ang), mag * jnp.sin(ang)


def _place(x, offset, width):
    c = x.shape[1]
    row = lax.broadcasted_iota(jnp.int32, (c, width), 0)
    lane = lax.broadcasted_iota(jnp.int32, (c, width), 1)
    return jnp.dot(x, (lane == row + offset).astype(F32), precision=HIGHEST, preferred_element_type=F32)


def _s5_prep_kernel(lrr_ref, lir_ref, lrc_ref, lic_ref, ldtr_ref, ldtc_ref, brt_ref, bit_ref, br_ref, bi_ref,
                    cr_ref, ci_ref, crt_ref, cit_ref, m_ref, w_ref, v_ref, are_ref, aim_ref):
    gpb, p, hh, c = S5_GPB, SSM_STATE, SSM_GROUP, S5_CHUNK
    sp = gpb * p

    lrr = jnp.concatenate([lrr_ref[g] for g in range(gpb)], axis=1)
    lir = jnp.concatenate([lir_ref[g] for g in range(gpb)], axis=1)
    dtr = jnp.exp(jnp.concatenate([jnp.broadcast_to(ldtr_ref[g], (1, p)) for g in range(gpb)], axis=1))
    lrc = lrc_ref[...].reshape(sp, 1)
    lic = lic_ref[...].reshape(sp, 1)
    dtc = jnp.exp(jnp.broadcast_to(ldtc_ref[...], (gpb, p, 1)).reshape(sp, 1))
    f_re_r, f_im_r = _s5_discretize(lrr, lir, dtr)
    f_re_c, f_im_c = _s5_discretize(lrc, lic, dtc)

    bt_re = jnp.concatenate([_place(brt_ref[g], g * p, sp) for g in range(gpb)], axis=0)
    bt_im = jnp.concatenate([_place(bit_ref[g], g * p, sp) for g in range(gpb)], axis=0)
    b_re = jnp.concatenate([_place(br_ref[g], g * hh, LANES) for g in range(gpb)], axis=0)
    b_im = jnp.concatenate([_place(bi_ref[g], g * hh, LANES) for g in range(gpb)], axis=0)
    c_re = jnp.concatenate([_place(cr_ref[g], g * p, sp) for g in range(gpb)], axis=0)
    c_im = jnp.concatenate([_place(ci_ref[g], g * p, sp) for g in range(gpb)], axis=0)
    ct_re = jnp.concatenate([_place(crt_ref[g], g * hh, LANES) for g in range(gpb)], axis=0)
    ct_im = jnp.concatenate([_place(cit_ref[g], g * hh, LANES) for g in range(gpb)], axis=0)
    bbt_re = f_re_r * bt_re - f_im_r * bt_im
    bbt_im = f_re_r * bt_im + f_im_r * bt_re
    bb_re = f_re_c * b_re - f_im_c * b_im
    bb_im = f_re_c * b_im + f_im_c * b_re

    e_row = lax.broadcasted_iota(jnp.int32, (c + 1, 1), 0).astype(F32)
    pr_r, pi_r = _s5_power(lrr, lir, dtr, e_row)
    e_lane = lax.broadcasted_iota(jnp.int32, (1, LANES), 1)
    e_lane = jnp.where(e_lane <= c, e_lane, 0).astype(F32)
    pr_c, pi_c = _s5_power(lrc, lic, dtc, e_lane)

    for s in range(c):
        pr, pi = pr_r[c - 1 - s:c - s], pi_r[c - 1 - s:c - s]
        w_ref[0, s * LANES:(s + 1) * LANES, :] = jnp.concatenate(
            [bbt_re * pr - bbt_im * pi, bbt_re * pi + bbt_im * pr], axis=1).astype(BF16)

    for t in range(c):
        pr = jnp.broadcast_to(pr_c[:, t + 1:t + 2], (sp, LANES))
        pi = jnp.broadcast_to(pi_c[:, t + 1:t + 2], (sp, LANES))
        v_ref[0, :sp, t * LANES:(t + 1) * LANES] = (ct_re * pr - ct_im * pi).astype(BF16)
        v_ref[0, sp:, t * LANES:(t + 1) * LANES] = (-(ct_re * pi + ct_im * pr)).astype(BF16)

    vj_re = jnp.concatenate([c_re * pr_r[j:j + 1] - c_im * pi_r[j:j + 1] for j in range(c)], axis=0)
    vj_im = jnp.concatenate([c_re * pi_r[j:j + 1] + c_im * pr_r[j:j + 1] for j in range(c)], axis=0)
    kt = (jnp.dot(vj_re, bb_re, precision=HIGHEST, preferred_element_type=F32)
          - jnp.dot(vj_im, bb_im, precision=HIGHEST, preferred_element_type=F32))
    strip = jnp.concatenate([kt[j * LANES:(j + 1) * LANES].T for j in range(c)], axis=1)
    for s in range(c):
        blk = strip if s == 0 else jnp.concatenate(
            [jnp.zeros((LANES, s * LANES), F32), strip[:, :(c - s) * LANES]], axis=1)
        m_ref[0, s * LANES:(s + 1) * LANES, :] = blk.astype(BF16)

    are_ref[...] = pr_r[c:c + 1]
    aim_ref[...] = pi_r[c:c + 1]


def _s5_prep(lam_re, lam_im, log_dt, b_re, b_im, c_re, c_im):
    g, p = lam_re.shape
    gpb, hh = S5_GPB, SSM_GROUP
    blk3 = lambda d1, d2: pl.BlockSpec((gpb, d1, d2), lambda j: (j, 0, 0))
    out3 = lambda d1, d2: pl.BlockSpec((1, d1, d2), lambda j: (j, 0, 0))
    arow = pl.BlockSpec((1, gpb * p), lambda j: (0, j))
    return pl.pallas_call(
        _s5_prep_kernel,
        out_shape=(jax.ShapeDtypeStruct((S5_BLOCKS, S5_K, S5_K), BF16),
                   jax.ShapeDtypeStruct((S5_BLOCKS, S5_K, 2 * S5_HALF), BF16),
                   jax.ShapeDtypeStruct((S5_BLOCKS, 2 * S5_HALF, S5_K), BF16),
                   jax.ShapeDtypeStruct((1, g * p), F32), jax.ShapeDtypeStruct((1, g * p), F32)),
        grid=(S5_BLOCKS,),
        in_specs=[blk3(1, p), blk3(1, p), blk3(p, 1), blk3(p, 1), blk3(1, 1), blk3(1, 1),
                  blk3(hh, p), blk3(hh, p), blk3(p, hh), blk3(p, hh),
                  blk3(hh, p), blk3(hh, p), blk3(p, hh), blk3(p, hh)],
        out_specs=(out3(S5_K, S5_K), out3(S5_K, 2 * S5_HALF), out3(2 * S5_HALF, S5_K), arow, arow),
        compiler_params=_params(("parallel",)),
        name="s5_chunk_operators",
    )(lam_re.reshape(g, 1, p), lam_im.reshape(g, 1, p), lam_re.reshape(g, p, 1), lam_im.reshape(g, p, 1),
      log_dt.reshape(g, 1, 1), log_dt.reshape(g, 1, 1),
      jnp.swapaxes(b_re, 1, 2), jnp.swapaxes(b_im, 1, 2), b_re, b_im,
      c_re, c_im, jnp.swapaxes(c_re, 1, 2), jnp.swapaxes(c_im, 1, 2))


S5_ROWS = 512
S5_OUT_ROWS = 256


def _s5_gather_chunks(u_ref, rows):
    return jnp.concatenate([u_ref[pl.ds(s, rows, stride=S5_CHUNK), :] for s in range(S5_CHUNK)], axis=1)


def _s5_local_kernel(u_ref, w_ref, x_ref, dre_ref, dim_ref):
    rows = x_ref.shape[0]
    x = _s5_gather_chunks(u_ref, rows).astype(BF16)
    x_ref[...] = x
    d = jnp.dot(x, w_ref[0], preferred_element_type=F32)
    dre_ref[...] = d[:, :S5_HALF]
    dim_ref[...] = d[:, S5_HALF:]


def _s5_local(u2, w_blk, rows):
    t = u2.shape[0]
    nc = t // S5_CHUNK
    return pl.pallas_call(
        _s5_local_kernel,
        out_shape=(jax.ShapeDtypeStruct((nc, S5_BLOCKS * S5_K), BF16),
                   jax.ShapeDtypeStruct((nc, S5_BLOCKS * S5_HALF), F32),
                   jax.ShapeDtypeStruct((nc, S5_BLOCKS * S5_HALF), F32)),
        grid=(S5_BLOCKS, nc // rows),
        in_specs=[pl.BlockSpec((rows * S5_CHUNK, LANES), lambda j, i: (i, j)),
                  pl.BlockSpec((1, S5_K, 2 * S5_HALF), lambda j, i: (j, 0, 0))],
        out_specs=(pl.BlockSpec((rows, S5_K), lambda j, i: (i, j)),
                   pl.BlockSpec((rows, S5_HALF), lambda j, i: (i, j)),
                   pl.BlockSpec((rows, S5_HALF), lambda j, i: (i, j))),
        compiler_params=_params(("parallel", "parallel")),
        name="s5_chunk_state_increment",
    )(u2, w_blk)


def _s5_scan_kernel(dre_ref, dim_ref, h0re_ref, h0im_ref, ar_ref, ai_ref, hre_ref, him_ref, fre_ref, fim_ref,
                    cre_ref, cim_ref):
    @pl.when(pl.program_id(2) == 0)
    def _():
        cre_ref[...] = h0re_ref[0]
        cim_ref[...] = h0im_ref[0]

    ar = ar_ref[...]
    ai = ai_ref[...]

    def body(r, h):
        hr, hi = h
        hre_ref[0, pl.ds(r, 1), :] = hr
        him_ref[0, pl.ds(r, 1), :] = hi
        dr = dre_ref[0, pl.ds(r, 1), :]
        di = dim_ref[0, pl.ds(r, 1), :]
        return ar * hr - ai * hi + dr, ar * hi + ai * hr + di

    hr, hi = lax.fori_loop(0, dre_ref.shape[1], body, (cre_ref[...], cim_ref[...]))
    cre_ref[...] = hr
    cim_ref[...] = hi
    fre_ref[0] = hr
    fim_ref[0] = hi


def _s5_scan(d_re, d_im, h0_re, h0_im, a_re, a_im, tc, tw):
    b, ncb, w = d_re.shape
    dspec = pl.BlockSpec((1, tc, tw), lambda i, k, c: (i, c, k))
    hspec = pl.BlockSpec((1, 1, tw), lambda i, k, c: (i, 0, k))
    aspec = pl.BlockSpec((1, tw), lambda i, k, c: (0, k))
    big = jax.ShapeDtypeStruct((b, ncb, w), F32)
    small = jax.ShapeDtypeStruct((b, 1, w), F32)
    return pl.pallas_call(
        _s5_scan_kernel,
        out_shape=(big, big, small, small),
        grid=(b, w // tw, ncb // tc),
        in_specs=[dspec, dspec, hspec, hspec, aspec, aspec],
        out_specs=(dspec, dspec, hspec, hspec),
        scratch_shapes=[pltpu.VMEM((1, tw), F32), pltpu.VMEM((1, tw), F32)],
        compiler_params=_params(("parallel", "parallel", "arbitrary")),
        name="s5_chunk_state_scan",
    )(d_re, d_im, h0_re, h0_im, a_re, a_im)


def _gelu_tanh(x):
    return x * (0.5 * (1.0 + jnp.tanh(math.sqrt(2.0 / math.pi) * (x + 0.044715 * (x * x * x)))))


def _s5_out_kernel(x_ref, hre_ref, him_ref, m_ref, v_ref, u_ref, d_ref, z_ref, z_scr):
    rows = x_ref.shape[0]
    hre = hre_ref[...].astype(BF16)
    him = him_ref[...].astype(BF16)
    skip = d_ref[...]
    pair = 2 * LANES
    for q in range(S5_K // pair):
        cols = slice(q * pair, (q + 1) * pair)
        k = (q + 1) * pair
        y = (jnp.dot(x_ref[:, :k], m_ref[0, :k, cols], preferred_element_type=F32)
             + jnp.dot(hre, v_ref[0, :S5_HALF, cols], preferred_element_type=F32)
             + jnp.dot(him, v_ref[0, S5_HALF:, cols], preferred_element_type=F32))
        for e in range(2):
            tok = pl.ds(2 * q + e, rows, stride=S5_CHUNK)
            z_scr[tok, :] = _gelu_tanh(y[:, e * LANES:(e + 1) * LANES] + skip * u_ref[tok, :])
    z_ref[...] = z_scr[...].astype(BF16)


def _s5_out(x, h_re, h_im, m_blk, v_blk, u2, ssm_d, rows):
    nc = x.shape[0]
    t = nc * S5_CHUNK
    return pl.pallas_call(
        _s5_out_kernel,
        out_shape=jax.ShapeDtypeStruct((t, SSM_WIDTH), BF16),
        grid=(S5_BLOCKS, nc // rows),
        in_specs=[pl.BlockSpec((rows, S5_K), lambda j, i: (i, j)),
                  pl.BlockSpec((rows, S5_HALF), lambda j, i: (i, j)),
                  pl.BlockSpec((rows, S5_HALF), lambda j, i: (i, j)),
                  pl.BlockSpec((1, S5_K, S5_K), lambda j, i: (j, 0, 0)),
                  pl.BlockSpec((1, 2 * S5_HALF, S5_K), lambda j, i: (j, 0, 0)),
                  pl.BlockSpec((rows * S5_CHUNK, LANES), lambda j, i: (i, j)),
                  pl.BlockSpec((1, LANES), lambda j, i: (0, j))],
        out_specs=pl.BlockSpec((rows * S5_CHUNK, LANES), lambda j, i: (i, j)),
        scratch_shapes=[pltpu.VMEM((rows * S5_CHUNK, LANES), F32)],
        compiler_params=_params(("parallel", "parallel")),
        name="s5_chunk_output",
    )(x, h_re, h_im, m_blk, v_blk, u2, ssm_d.reshape(1, SSM_WIDTH))


def _state_to_blocks(h):
    return h.astype(F32).reshape(h.shape[0], 1, N_GROUPS * SSM_STATE)


def _s5(u, h0_re, h0_im, blk_ops, ssm_d):
    m_blk, w_blk, v_blk, a_re, a_im = blk_ops
    b, l, _ = u.shape
    ncb = l // S5_CHUNK
    w = N_GROUPS * SSM_STATE
    u2 = u.reshape(b * l, SSM_WIDTH)
    rows = min(b * ncb, S5_ROWS)
    x, d_re, d_im = _s5_local(u2, w_blk, rows)
    if h0_re is None:
        h0_re = h0_im = jnp.zeros((b, N_GROUPS, SSM_STATE), F32)
    hs_re, hs_im, hf_re, hf_im = _s5_scan(d_re.reshape(b, ncb, w), d_im.reshape(b, ncb, w),
                                          _state_to_blocks(h0_re), _state_to_blocks(h0_im), a_re, a_im,
                                          tc=min(ncb, 128), tw=w if ncb <= 8 else 2048)
    z = _s5_out(x, hs_re.reshape(b * ncb, w), hs_im.reshape(b * ncb, w), m_blk, v_blk, u2, ssm_d,
                min(b * ncb, S5_OUT_ROWS))
    return (z.reshape(b, l, SSM_WIDTH), hf_re.reshape(b, N_GROUPS, SSM_STATE), hf_im.reshape(b, N_GROUPS, SSM_STATE))


def _mix_kernel(x_ref, z_ref, o_ref, gate_ref, sc_ref, sh_ref, g_ref, wg_ref, bg_ref, wo_ref, out_ref, h_ref):
    nb, tl, d = x_ref.shape
    rows = nb * tl
    gl = jnp.dot(z_ref[...].reshape(rows, SSM_WIDTH), wg_ref[...], preferred_element_type=F32) + bg_ref[...]
    yg = (gl[:, :SSM_WIDTH] * _sigmoid(gl[:, SSM_WIDTH:])).astype(BF16)
    mix = (jnp.dot(o_ref[...].reshape(rows, ATTN_WIDTH), wo_ref[:ATTN_WIDTH, :], preferred_element_type=F32)
           + jnp.dot(yg, wo_ref[ATTN_WIDTH:, :], preferred_element_type=F32))
    x1 = x_ref[...] + gate_ref[...] * mix.reshape(nb, tl, d)
    out_ref[...] = x1
    h_ref[...] = _modulated_norm(x1, g_ref[...], sc_ref[...], sh_ref[...]).astype(BF16)


def _mix(x, z_ssm, o_attn, gate, scale2, shift2, g_ffn, w_glu_b, b_glu, w_out_b, nb, tl):
    b, l, d = x.shape
    xspec = lambda w: pl.BlockSpec((nb, tl, w), lambda i, j: (i, j, 0))
    mspec = pl.BlockSpec((nb, 1, d), lambda i, j: (i, 0, 0))
    const = lambda shape: pl.BlockSpec(shape, lambda i, j: (0, 0), pipeline_mode=pl.Buffered(1))
    return pl.pallas_call(
        _mix_kernel,
        out_shape=(jax.ShapeDtypeStruct((b, l, d), F32), jax.ShapeDtypeStruct((b, l, d), BF16)),
        grid=(b // nb, l // tl),
        in_specs=[xspec(d), xspec(SSM_WIDTH), xspec(ATTN_WIDTH), mspec, mspec, mspec,
                  pl.BlockSpec((1, d), lambda i, j: (0, 0)),
                  const((SSM_WIDTH, 2 * SSM_WIDTH)), const((1, 2 * SSM_WIDTH)), const((d, d))],
        out_specs=(xspec(d), xspec(d)),
        compiler_params=_params(("parallel", "parallel")),
        name="glu_outproj_residual",
    )(x, z_ssm, o_attn, gate, scale2, shift2, g_ffn.reshape(1, d), w_glu_b, b_glu.reshape(1, -1), w_out_b)


FFN_ROWS = 512
FFN_TF = 1024

def _ffn_kernel(x_ref, h_ref, gate_ref, gf_ref, w1_ref, w2_ref, y_ref, acc_ref):
    nb, tl, d = x_ref.shape
    j = pl.program_id(2)

    @pl.when(j == 0)
    def _():
        acc_ref[...] = jnp.zeros_like(acc_ref)

    hid = jnp.dot(h_ref[...].reshape(nb * tl, d), w1_ref[...], preferred_element_type=F32)
    hid = jnp.square(jnp.maximum(hid, 0.0)).astype(BF16)
    acc_ref[...] += jnp.dot(hid, w2_ref[...], preferred_element_type=F32)

    @pl.when(j == pl.num_programs(2) - 1)
    def _():
        x = x_ref[...] + gate_ref[...] * acc_ref[...].reshape(nb, tl, d)
        ms = jnp.mean(x * x, axis=-1, keepdims=True)
        y_ref[...] = x * lax.rsqrt(ms + EPS) * gf_ref[...]


def _ffn(x, h, gate, g_final, w1_b, w2_b, nb, tl, tf):
    b, l, d = x.shape
    xspec = pl.BlockSpec((nb, tl, d), lambda i, j, k: (i, j, 0))
    mspec = pl.BlockSpec((nb, 1, d), lambda i, j, k: (i, 0, 0))
    gspec = pl.BlockSpec((1, d), lambda i, j, k: (0, 0))
    return pl.pallas_call(
        _ffn_kernel,
        out_shape=jax.ShapeDtypeStruct((b, l, d), F32),
        grid=(b // nb, l // tl, D_FF // tf),
        in_specs=[xspec, xspec, mspec, gspec,
                  pl.BlockSpec((d, tf), lambda i, j, k: (0, k)),
                  pl.BlockSpec((tf, d), lambda i, j, k: (k, 0))],
        out_specs=xspec,
        scratch_shapes=[pltpu.VMEM((nb * tl, d), F32)],
        compiler_params=_params(("parallel", "parallel", "arbitrary")),
        name="relu2_mlp_final_norm",
    )(x, h, gate, g_final.reshape(1, d), w1_b, w2_b)


def _layer(x, mods, tables, tables_t, past_kt, past_v2, h0_re, h0_im, wts, nb, tl):
    shift1, scale1, gate1, shift2, scale2, gate2 = mods
    b, l, _ = x.shape
    first_chunk = past_kt is None
    w_k = wts["w_k_t"] if first_chunk else wts["w_k"]
    q, k, v, u, kb, vb = _inproj(x, scale1, shift1, wts["g_mix"], wts["w_qvu"], w_k, tables, tables_t, nb, tl)
    if first_chunk:
        o = _attn_prompt(q, kb, vb, wts["lam_vecs"], wts["g_subln"])
        k6 = k.reshape(b, N_HEADS, 2, HEAD_DIM, l).transpose(0, 4, 1, 2, 3)[None]
    else:
        o = _attn_sample(q, past_kt, past_v2, kb, vb, wts["lam_vecs"], wts["g_subln"])
        k6 = k.reshape(1, b, l, N_HEADS, 2, HEAD_DIM)
    z_ssm, h_re, h_im = _s5(u, h0_re, h0_im, wts["s5_ops"], wts["ssm_d"])
    x1, h2 = _mix(x, z_ssm, o, gate1, scale2, shift2, wts["g_ffn"], wts["w_glu"], wts["b_glu"], wts["w_out"], nb, tl)
    tl_ffn = min(l, FFN_ROWS) if nb == 1 else tl
    y = _ffn(x1, h2, gate2, wts["g_final"], wts["w_ff1"], wts["w_ff2"], nb, tl_ffn, tf=FFN_TF)
    v5 = v.reshape(1, b, l, N_HEADS, V_DIM)
    return y, k6, v5, h_re[None], h_im[None]


def kernel(x_prompt, x_sample, c_prompt, c_sample, cache_k, cache_v, state_ssm_re, state_ssm_im, w_ada, b_ada, g_mix, w_in, lam_q1, lam_k1, lam_q2, lam_k2, g_subln, ssm_lam_re, ssm_lam_im, ssm_log_dt, ssm_b_re, ssm_b_im, ssm_c_re, ssm_c_im, ssm_d, w_glu, b_glu, w_out, g_ffn, w_ff1, w_ff2, g_final):
    bp, lp, d = x_prompt.shape
    bs, ls, _ = x_sample.shape
    past = cache_k.shape[2]

    m = _ada(jnp.concatenate([c_prompt, c_sample], axis=0), w_ada[0], b_ada[0])
    mods = [m[:, i * d:(i + 1) * d].reshape(bp + bs, 1, d) for i in range(6)]
    mods_p = [t[:bp] for t in mods]
    mods_s = [t[bp:] for t in mods]

    w_in_b = w_in[0].astype(BF16)
    w_k = w_in_b[:, ATTN_WIDTH:2 * ATTN_WIDTH]
    wts = {
        "g_mix": g_mix[0],
        "w_qvu": jnp.concatenate([w_in_b[:, :ATTN_WIDTH], w_in_b[:, 2 * ATTN_WIDTH:]], axis=1),
        "w_k": w_k, "w_k_t": w_k.T,
        "lam_vecs": tuple(t[0].reshape(1, HEAD_DIM) for t in (lam_q1, lam_k1, lam_q2, lam_k2)),
        "g_subln": g_subln[0],
        "s5_ops": _s5_prep(ssm_lam_re[0], ssm_lam_im[0], ssm_log_dt[0], ssm_b_re[0], ssm_b_im[0],
                           ssm_c_re[0], ssm_c_im[0]),
        "ssm_d": ssm_d[0], "w_glu": w_glu[0].astype(BF16), "b_glu": b_glu[0],
        "w_out": w_out[0].astype(BF16), "g_ffn": g_ffn[0], "g_final": g_final,
        "w_ff1": w_ff1[0].astype(BF16), "w_ff2": w_ff2[0].astype(BF16),
    }

    tl_p = min(lp, 512)
    yp, kp, vp, srp, sip = _layer(x_prompt, mods_p, _rope_tables(lp, 0), _rope_tables_t(lp, 0),
                                  None, None, None, None, wts, nb=1, tl=tl_p)
    nb_s = max(1, min(bs, 512 // ls))
    ckt = cache_k[0].reshape(bs, past, ATTN_WIDTH).transpose(0, 2, 1)
    cv2 = cache_v[0].reshape(bs, past * N_HEADS, V_DIM)
    ys, ks, vs, srs, sis = _layer(x_sample, mods_s, _rope_tables(ls, past), None, ckt, cv2,
                                  state_ssm_re[0], state_ssm_im[0], wts, nb=nb_s, tl=ls)
    return (yp, ys, kp, vp, srp, sip, ks, vs, srs, sis)
```
